```python
import math
import jax, jax.numpy as jnp
from jax import lax
import numpy as np

D_MODEL = 4096
BATCH = 4
SEQ = 2048
DEPTH = 2
DEC_BATCH = 8
DEC_SEQ = 1
PAST_LEN = 16384
PAGE_SIZE = 128

N_A_LAYERS = DEPTH // 2
N_B_LAYERS = DEPTH - N_A_LAYERS
HEAD_DIM = 128
N_HEADS = D_MODEL // HEAD_DIM
N_KV_HEADS = max(1, N_HEADS // 4)
GROUP = N_HEADS // N_KV_HEADS
ROT_DIM = HEAD_DIM // 4
ROPE_THETA = 500000.0
MOBA_BLOCK = 256
MOBA_TOP_K = 3
Q_CHUNK = 8
CONV_WIDTH = 3
D_CONV = D_MODEL
D_FF = -(-8 * D_MODEL // (3 * 256)) * 256
RMS_EPS = 1e-6
NEG_INF = -1e30

kernel_name = "yoco_shortconv_moba_decoder_step"


def rmsnorm(x, g):
    xf = x.astype(jnp.float32)
    xf = xf * lax.rsqrt(jnp.mean(xf * xf, axis=-1, keepdims=True) + RMS_EPS)
    return (xf * g.astype(jnp.float32)).astype(x.dtype)


def partial_rotary(x, pos):
    half = ROT_DIM // 2
    inv_freq = ROPE_THETA ** (-jnp.arange(half, dtype=jnp.float32) * 2.0 / ROT_DIM)
    ang = pos.astype(jnp.float32)[:, None] * inv_freq[None, :]
    cos = jnp.cos(ang)[None, :, None, :]
    sin = jnp.sin(ang)[None, :, None, :]
    xf = x.astype(jnp.float32)
    x1 = xf[..., :half]
    x2 = xf[..., half:ROT_DIM]
    out = jnp.concatenate([x1 * cos - x2 * sin, x2 * cos + x1 * sin, xf[..., ROT_DIM:]], axis=-1)
    return out.astype(x.dtype)


def short_conv_mixer(x, conv_state, w_in, conv_w, w_out):
    T = x.shape[1]
    b_gate, c_gate, x_in = jnp.split(x @ w_in, 3, axis=-1)
    u = c_gate * x_in
    u_ext = jnp.concatenate([conv_state.astype(u.dtype), u], axis=1)
    conv = conv_w[0] * u_ext[:, 0:T]
    for j in range(1, CONV_WIDTH):
        conv = conv + conv_w[j] * u_ext[:, j:j + T]
    y = (b_gate * conv) @ w_out
    return y, u_ext[:, T:]


def swiglu(x, w_gate, w_up, w_down):
    return (jax.nn.silu(x @ w_gate) * (x @ w_up)) @ w_down


def moba_attention(q, k_all, v_all, q_pos):
    B, T, H, Dh = q.shape
    L = k_all.shape[1]
    nb = -(-L // MOBA_BLOCK)
    pad = nb * MOBA_BLOCK - L
    kb = jnp.pad(k_all, ((0, 0), (0, pad), (0, 0), (0, 0))).reshape(B, nb, MOBA_BLOCK, N_KV_HEADS, Dh)
    vb = jnp.pad(v_all, ((0, 0), (0, pad), (0, 0), (0, 0))).reshape(B, nb, MOBA_BLOCK, N_KV_HEADS, Dh)
    k_mean = jnp.mean(kb.astype(jnp.float32), axis=2)
    gate = jnp.einsum('btkgd,bnkd->btkgn',
                      q.astype(jnp.float32).reshape(B, T, N_KV_HEADS, GROUP, Dh), k_mean)
    gate = gate.reshape(B, T, H, nb)
    own = q_pos // MOBA_BLOCK
    fully_past = jnp.arange(nb)[None, :] < own[:, None]
    gate = jnp.where(fully_past[None, :, None, :], gate, NEG_INF)
    k_sel = min(MOBA_TOP_K, nb)
    _, sel = lax.top_k(gate, k_sel)
    sel_ok = jnp.arange(k_sel)[None, :] < jnp.minimum(own, MOBA_TOP_K)[:, None]

    c = math.gcd(T, Q_CHUNK)
    nc = T // c

    def to_chunks(a):
        a = jnp.moveaxis(a, 1, 0)
        return a.reshape((nc, c) + a.shape[1:])

    xs = (to_chunks(q), to_chunks(sel), sel_ok.reshape(nc, c, k_sel), q_pos.reshape(nc, c))
    b_idx = jnp.arange(B)[None, :, None, None]
    h_idx = (jnp.arange(H) // GROUP)[None, None, :, None]
    scale = HEAD_DIM ** -0.5

    def chunk_attend(args):
        q_c, sel_c, ok_c, pos_c = args
        qf = q_c.astype(jnp.float32) * scale
        k_g = kb[b_idx, sel_c, :, h_idx, :].astype(jnp.float32)
        v_g = vb[b_idx, sel_c, :, h_idx, :].astype(jnp.float32)
        s_sel = jnp.einsum('cbhd,cbhskd->cbhsk', qf, k_g)
        s_sel = jnp.where(ok_c[:, None, None, :, None], s_sel, NEG_INF).reshape(c, B, H, k_sel * MOBA_BLOCK)
        own_c = pos_c[0] // MOBA_BLOCK
        k_o = lax.dynamic_index_in_dim(kb, own_c, axis=1, keepdims=False).astype(jnp.float32)
        v_o = lax.dynamic_index_in_dim(vb, own_c, axis=1, keepdims=False).astype(jnp.float32)
        s_own = jnp.einsum('cbkgd,bnkd->cbkgn', qf.reshape(c, B, N_KV_HEADS, GROUP, Dh), k_o)
        s_own = s_own.reshape(c, B, H, MOBA_BLOCK)
        k_pos = own_c * MOBA_BLOCK + jnp.arange(MOBA_BLOCK)
        causal = (k_pos[None, :] <= pos_c[:, None])[:, None, None, :]
        s_own = jnp.where(causal, s_own, NEG_INF)
        p = jax.nn.softmax(jnp.concatenate([s_sel, s_own], axis=-1), axis=-1)
        p_sel = p[..., :k_sel * MOBA_BLOCK].reshape(c, B, H, k_sel, MOBA_BLOCK)
        p_own = p[..., k_sel * MOBA_BLOCK:].reshape(c, B, N_KV_HEADS, GROUP, MOBA_BLOCK)
        o = (jnp.einsum('cbhsk,cbhskd->cbhd', p_sel, v_g)
             + jnp.einsum('cbkgn,bnkd->cbkgd', p_own, v_o).reshape(c, B, H, Dh))
        return o.astype(q.dtype)

    out = lax.map(chunk_attend, xs)
    return jnp.moveaxis(out.reshape(T, B, H, Dh), 0, 1)


def decoder_trunk(x, pos, conv_state, past_k, past_v, a_norm, a_w_in, a_conv_w, a_w_out,
                  kv_norm, w_k, w_v, b_norm, w_q, w_o, ffn_norm, w_gate, w_up, w_down, final_norm):
    B, T, _ = x.shape
    h = x
    conv_states = []
    for layer in range(DEPTH):
        if layer < N_A_LAYERS:
            y, st = short_conv_mixer(rmsnorm(h, a_norm[layer]), conv_state[layer],
                                     a_w_in[layer], a_conv_w[layer], a_w_out[layer])
            conv_states.append(st)
        else:
            if layer == N_A_LAYERS:
                kv_in = rmsnorm(h, kv_norm)
                k_new = partial_rotary((kv_in @ w_k).reshape(B, T, N_KV_HEADS, HEAD_DIM), pos)
                v_new = (kv_in @ w_v).reshape(B, T, N_KV_HEADS, HEAD_DIM)
                k_all = jnp.concatenate([past_k.astype(k_new.dtype), k_new], axis=1)
                v_all = jnp.concatenate([past_v.astype(v_new.dtype), v_new], axis=1)
            j = layer - N_A_LAYERS
            q = partial_rotary((rmsnorm(h, b_norm[j]) @ w_q[j]).reshape(B, T, N_HEADS, HEAD_DIM), pos)
            y = moba_attention(q, k_all, v_all, pos).reshape(B, T, D_MODEL) @ w_o[j]
        h = h + y
        h = h + swiglu(rmsnorm(h, ffn_norm[layer]), w_gate[layer], w_up[layer], w_down[layer])
    return rmsnorm(h, final_norm), jnp.stack(conv_states), k_new, v_new


def setup_inputs(seed: int = 0) -> dict:
    key = jax.random.key(seed)
    ks = jax.random.split(key, 24)
    f32 = jnp.float32
    n_pages = PAST_LEN // PAGE_SIZE
    n_used = DEC_BATCH * n_pages
    n_pool = n_used + -(-n_used // 4)
    kvw = N_KV_HEADS * HEAD_DIM

    def nrm(k, shape, fan_in):
        return jax.random.normal(k, shape, f32) * (fan_in ** -0.5)

    def gain(k, shape):
        return 1.0 + 0.02 * jax.random.normal(k, shape, f32)

    page_table = jax.random.permutation(ks[5], n_pool)[:n_used].reshape(DEC_BATCH, n_pages).astype(jnp.int32)
    return {
        "x_prompt": jax.random.normal(ks[0], (BATCH, SEQ, D_MODEL), f32),
        "x_sample": jax.random.normal(ks[1], (DEC_BATCH, DEC_SEQ, D_MODEL), f32),
        "state_conv": jax.random.normal(ks[2], (N_A_LAYERS, DEC_BATCH, CONV_WIDTH - 1, D_CONV), f32),
        "cache_k": jax.random.normal(ks[3], (n_pool, PAGE_SIZE, N_KV_HEADS, HEAD_DIM), f32),
        "cache_v": jax.random.normal(ks[4], (n_pool, PAGE_SIZE, N_KV_HEADS, HEAD_DIM), f32),
        "page_table": page_table,
        "a_norm": gain(ks[6], (N_A_LAYERS, D_MODEL)),
        "a_w_in": nrm(ks[7], (N_A_LAYERS, D_MODEL, 3 * D_CONV), D_MODEL),
        "a_conv_w": nrm(ks[8], (N_A_LAYERS, CONV_WIDTH, D_CONV), CONV_WIDTH),
        "a_w_out": nrm(ks[9], (N_A_LAYERS, D_CONV, D_MODEL), D_CONV),
        "kv_norm": gain(ks[10], (D_MODEL,)),
        "w_k": nrm(ks[11], (D_MODEL, kvw), D_MODEL),
        "w_v": nrm(ks[12], (D_MODEL, kvw), D_MODEL),
        "b_norm": gain(ks[13], (N_B_LAYERS, D_MODEL)),
        "w_q": nrm(ks[14], (N_B_LAYERS, D_MODEL, N_HEADS * HEAD_DIM), D_MODEL),
        "w_o": nrm(ks[15], (N_B_LAYERS, N_HEADS * HEAD_DIM, D_MODEL), N_HEADS * HEAD_DIM),
        "ffn_norm": gain(ks[16], (DEPTH, D_MODEL)),
        "w_gate": nrm(ks[17], (DEPTH, D_MODEL, D_FF), D_MODEL),
        "w_up": nrm(ks[18], (DEPTH, D_MODEL, D_FF), D_MODEL),
        "w_down": nrm(ks[19], (DEPTH, D_FF, D_MODEL), D_FF),
        "final_norm": gain(ks[20], (D_MODEL,)),
    }


def reference(x_prompt, x_sample, state_conv, cache_k, cache_v, page_table, a_norm, a_w_in, a_conv_w,
              a_w_out, kv_norm, w_k, w_v, b_norm, w_q, w_o, ffn_norm, w_gate, w_up, w_down, final_norm):
    weights = (a_norm, a_w_in, a_conv_w, a_w_out, kv_norm, w_k, w_v, b_norm, w_q, w_o,
               ffn_norm, w_gate, w_up, w_down, final_norm)
    bp, tp, _ = x_prompt.shape
    pos_p = jnp.arange(tp, dtype=jnp.int32)
    zero_conv = jnp.zeros((N_A_LAYERS, bp, CONV_WIDTH - 1, D_CONV), x_prompt.dtype)
    no_past = jnp.zeros((bp, 0, N_KV_HEADS, HEAD_DIM), x_prompt.dtype)
    y_prompt, conv_prompt, k_prompt, v_prompt = decoder_trunk(
        x_prompt, pos_p, zero_conv, no_past, no_past, *weights)
    db, n_pages = page_table.shape
    past_len = n_pages * cache_k.shape[1]
    past_k = cache_k[page_table].reshape(db, past_len, N_KV_HEADS, HEAD_DIM)
    past_v = cache_v[page_table].reshape(db, past_len, N_KV_HEADS, HEAD_DIM)
    pos_s = past_len + jnp.arange(x_sample.shape[1], dtype=jnp.int32)
    y_sample, conv_sample, k_sample, v_sample = decoder_trunk(
        x_sample, pos_s, state_conv, past_k, past_v, *weights)
    return (y_prompt, y_sample, conv_prompt, conv_sample, k_prompt, v_prompt, k_sample, v_sample)
```

```python
import functools
import math

import jax
import jax.numpy as jnp
from jax import lax
from jax.experimental import pallas as pl
from jax.experimental.pallas import tpu as pltpu

HEAD_DIM = 128
GROUP = 4
ROT_DIM = HEAD_DIM // 4
ROT_HALF = ROT_DIM // 2
ROPE_THETA = 500000.0
MOBA_BLOCK = 256
MOBA_TOP_K = 3
CONV_WIDTH = 3
RMS_EPS = 1e-6
NEG_INF = -1e30

LANES = 128
BF16_SUBLANES = 16
VMEM_LIMIT_BYTES = 58 * 1024 * 1024

ROW_PAD = 128


def _row_tile(rows, target):
    return max(t for t in range(BF16_SUBLANES, target + 1, BF16_SUBLANES) if rows % t == 0)


def _cparams(semantics):
    return pltpu.CompilerParams(dimension_semantics=semantics, vmem_limit_bytes=VMEM_LIMIT_BYTES)


def _rmsnorm_kernel(x_ref, g_ref, *o_refs):
    x = x_ref[...]
    xn = x * lax.rsqrt(jnp.mean(x * x, axis=-1, keepdims=True) + RMS_EPS)
    for i, o_ref in enumerate(o_refs):
        o_ref[...] = (xn * g_ref[i:i + 1, :]).astype(o_ref.dtype)


def rmsnorm(x, gains, out_dtypes, *, rows_per_step):
    rows, d = x.shape
    n_out = gains.shape[0]
    row_spec = pl.BlockSpec((rows_per_step, d), lambda r: (r, 0))
    return pl.pallas_call(
        _rmsnorm_kernel,
        grid=(rows // rows_per_step,),
        in_specs=[row_spec, pl.BlockSpec((n_out, d), lambda r: (0, 0))],
        out_specs=[row_spec] * n_out,
        out_shape=[jax.ShapeDtypeStruct((rows, d), dt) for dt in out_dtypes],
        compiler_params=_cparams(("parallel",)),
        name="rmsnorm",
    )(x, gains)


def _rotary_epilogue(y, cos, sin_signed):
    lane = lax.broadcasted_iota(jnp.int32, (y.shape[0], HEAD_DIM), 1)
    first_half = lane < ROT_HALF
    outs = []
    for h in range(y.shape[1] // HEAD_DIM):
        yh = y[:, h * HEAD_DIM:(h + 1) * HEAD_DIM]
        from_above = pltpu.roll(yh, HEAD_DIM - ROT_HALF, axis=1)
        from_below = pltpu.roll(yh, ROT_HALF, axis=1)
        partner = jnp.where(first_half, from_above, from_below)
        outs.append(yh * cos + partner * sin_signed)
    return outs[0] if len(outs) == 1 else jnp.concatenate(outs, axis=1)


def _matmul_kernel(*refs, n_w, n_k, mode):
    x_ref = refs[0]
    w_refs = refs[1:1 + n_w]
    pos = 1 + n_w
    extra = ()
    if mode == "residual":
        extra = refs[pos:pos + 1]
        pos += 1
    elif mode == "rotary":
        extra = refs[pos:pos + 2]
        pos += 2
    o_ref = refs[pos]
    acc_refs = refs[pos + 1:]

    x = x_ref[...]
    parts = [jnp.dot(x, w_ref[...], preferred_element_type=jnp.float32) for w_ref in w_refs]

    def finish(vals):
        if mode == "swiglu":
            g, u = vals
            y = (g * jax.nn.sigmoid(g)) * u
        elif mode == "residual":
            y = extra[0][...] + vals[0]
        elif mode == "rotary":
            y = _rotary_epilogue(vals[0], extra[0][...], extra[1][...])
        else:
            y = vals[0]
        o_ref[...] = y.astype(o_ref.dtype)

    if n_k == 1:
        finish(parts)
        return

    k = pl.program_id(2)

    @pl.when(k == 0)
    def _():
        for acc_ref, p in zip(acc_refs, parts):
            acc_ref[...] = p

    if n_k > 2:
        @pl.when(jnp.logical_and(k > 0, k < n_k - 1))
        def _():
            for acc_ref, p in zip(acc_refs, parts):
                acc_ref[...] += p

    @pl.when(k == n_k - 1)
    def _():
        finish([acc_ref[...] + p for acc_ref, p in zip(acc_refs, parts)])


def matmul(x, ws, *, tm, tn, tk=None, mode="plain", extra=(), out_dtype=jnp.float32, name="matmul"):
    m, k_dim = x.shape
    n = ws[0].shape[1]
    tn = min(tn, n)
    tk = k_dim if tk is None else tk
    n_k = k_dim // tk
    assert m % tm == 0 and n % tn == 0 and k_dim % tk == 0
    in_specs = [pl.BlockSpec((tm, tk), lambda i, j, k: (i, k))]
    in_specs += [pl.BlockSpec((tk, tn), lambda i, j, k: (k, j)) for _ in ws]
    if mode == "residual":
        in_specs.append(pl.BlockSpec((tm, tn), lambda i, j, k: (i, j)))
    elif mode == "rotary":
        in_specs += [pl.BlockSpec((tm, HEAD_DIM), lambda i, j, k: (i, 0))] * 2
    scratch = [pltpu.VMEM((tm, tn), jnp.float32) for _ in ws] if n_k > 1 else []
    return pl.pallas_call(
        functools.partial(_matmul_kernel, n_w=len(ws), n_k=n_k, mode=mode),
        grid=(m // tm, n // tn, n_k),
        in_specs=in_specs,
        out_specs=pl.BlockSpec((tm, tn), lambda i, j, k: (i, j)),
        out_shape=jax.ShapeDtypeStruct((m, n), out_dtype),
        scratch_shapes=scratch,
        compiler_params=_cparams(("parallel", "parallel", "arbitrary")),
        name=name,
    )(x, *ws, *extra)


def _conv_prompt_kernel(b_ref, c_ref, xin_ref, w_ref, g_ref, state_ref):
    u = c_ref[...] * xin_ref[...]
    t = lax.broadcasted_iota(jnp.int32, u.shape, 0)
    u_prev1 = jnp.where(t >= 1, pltpu.roll(u, 1, axis=0), 0.0)
    u_prev2 = jnp.where(t >= 2, pltpu.roll(u, 2, axis=0), 0.0)
    conv = w_ref[0:1, :] * u_prev2
    conv = conv + w_ref[1:2, :] * u_prev1
    conv = conv + w_ref[2:3, :] * u
    g_ref[...] = (b_ref[...] * conv).astype(g_ref.dtype)
    tail = pl.ds(u.shape[0] - (CONV_WIDTH - 1), CONV_WIDTH - 1)
    state_ref[...] = c_ref[tail, :] * xin_ref[tail, :]


def _conv_sample_kernel(g_in_ref, b_ref, c_ref, xin_ref, st0_ref, st1_ref, w_ref, g_ref, u_ref):
    del g_in_ref
    u = c_ref[...] * xin_ref[...]
    conv = w_ref[0:1, :] * st0_ref[...]
    conv = conv + w_ref[1:2, :] * st1_ref[...]
    conv = conv + w_ref[2:3, :] * u
    g_ref[...] = (b_ref[...] * conv).astype(g_ref.dtype)
    u_ref[...] = u


def conv_gate(bcx, conv_w, state_sample, *, n_batch, seq, n_sample, tc):
    rows, d3 = bcx.shape
    d = d3 // 3
    n_c = d // tc
    pad = jnp.zeros((ROW_PAD - n_sample, d), jnp.float32)
    st0 = jnp.concatenate([state_sample[:, 0, :], pad], axis=0)
    st1 = jnp.concatenate([state_sample[:, 1, :], pad], axis=0)
    g, st_prompt = pl.pallas_call(
        _conv_prompt_kernel,
        grid=(n_batch, n_c),
        in_specs=[pl.BlockSpec((seq, tc), lambda b, j: (b, j)),
                  pl.BlockSpec((seq, tc), lambda b, j: (b, n_c + j)),
                  pl.BlockSpec((seq, tc), lambda b, j: (b, 2 * n_c + j)),
                  pl.BlockSpec((CONV_WIDTH, tc), lambda b, j: (0, j))],
        out_specs=[pl.BlockSpec((seq, tc), lambda b, j: (b, j)),
                   pl.BlockSpec((None, CONV_WIDTH - 1, tc), lambda b, j: (b, 0, j))],
        out_shape=[jax.ShapeDtypeStruct((rows, d), jnp.bfloat16),
                   jax.ShapeDtypeStruct((n_batch, CONV_WIDTH - 1, d), jnp.float32)],
        compiler_params=_cparams(("parallel", "parallel")),
        name="conv_prompt",
    )(bcx, bcx, bcx, conv_w)

    sample_blk = (n_batch * seq) // ROW_PAD
    g, u_sample = pl.pallas_call(
        _conv_sample_kernel,
        grid=(n_c,),
        in_specs=[pl.BlockSpec(memory_space=pl.ANY),
                  pl.BlockSpec((ROW_PAD, tc), lambda j: (sample_blk, j)),
                  pl.BlockSpec((ROW_PAD, tc), lambda j: (sample_blk, n_c + j)),
                  pl.BlockSpec((ROW_PAD, tc), lambda j: (sample_blk, 2 * n_c + j)),
                  pl.BlockSpec((ROW_PAD, tc), lambda j: (0, j)),
                  pl.BlockSpec((ROW_PAD, tc), lambda j: (0, j)),
                  pl.BlockSpec((CONV_WIDTH, tc), lambda j: (0, j))],
        out_specs=[pl.BlockSpec((ROW_PAD, tc), lambda j: (sample_blk, j)),
                   pl.BlockSpec((ROW_PAD, tc), lambda j: (0, j))],
        out_shape=[jax.ShapeDtypeStruct((rows, d), jnp.bfloat16),
                   jax.ShapeDtypeStruct((ROW_PAD, d), jnp.float32)],
        input_output_aliases={0: 0},
        compiler_params=_cparams(("parallel",)),
        name="conv_sample",
    )(g, bcx, bcx, bcx, st0, st1, conv_w)
    st_sample = jnp.stack([state_sample[:, 1, :], u_sample[:n_sample]], axis=1)
    return g, st_prompt, st_sample


def _moba_prompt_kernel(q_ref, k_ref, v_ref, o_ref, kbf_ref, vbf_ref, kmean_ref, m_ref, l_ref, acc_ref,
                        *, n_blocks):
    own = pl.program_id(2)
    blk = MOBA_BLOCK
    rows = GROUP * blk

    @pl.when(own == 0)
    def _():
        k = k_ref[...]
        kbf_ref[...] = k.astype(jnp.bfloat16)
        vbf_ref[...] = v_ref[...].astype(jnp.bfloat16)
        for n in range(n_blocks):
            kmean_ref[n:n + 1, :] = jnp.mean(k[n * blk:(n + 1) * blk, :], axis=0, keepdims=True)

    q = jnp.concatenate([q_ref[:, g * HEAD_DIM:(g + 1) * HEAD_DIM] for g in range(GROUP)], axis=0)

    gate_t = lax.dot_general(kmean_ref[...], q, (((1,), (1,)), ((), ())),
                             precision=lax.Precision.HIGHEST, preferred_element_type=jnp.float32)
    n_iota = lax.broadcasted_iota(jnp.int32, gate_t.shape, 0)
    beaten_by = jnp.zeros(gate_t.shape, jnp.int32)
    for n in range(n_blocks):
        row = gate_t[n:n + 1, :]
        beats = jnp.logical_or(row > gate_t, jnp.logical_and(row == gate_t, n < n_iota))
        beaten_by = beaten_by + jnp.where(beats, 1, 0) * (n < own).astype(jnp.int32)
    sel_t = jnp.logical_and(beaten_by < MOBA_TOP_K, n_iota < own)
    sel = jnp.transpose(jnp.where(sel_t, 1.0, 0.0))

    qs = (q * (HEAD_DIM ** -0.5)).astype(jnp.bfloat16)
    m_ref[...] = jnp.full(m_ref.shape, -jnp.inf, jnp.float32)
    l_ref[...] = jnp.zeros(l_ref.shape, jnp.float32)
    acc_ref[...] = jnp.zeros(acc_ref.shape, jnp.float32)

    def attend(k_blk, v_blk, keep):
        s = lax.dot_general(qs, k_blk, (((1,), (1,)), ((), ())), preferred_element_type=jnp.float32)
        s = jnp.where(keep, s, NEG_INF)
        m_prev = m_ref[...]
        m_new = jnp.maximum(m_prev, jnp.max(s, axis=-1, keepdims=True))
        alpha = jnp.exp(m_prev - m_new)
        p = jnp.exp(s - m_new)
        l_ref[...] = alpha * l_ref[...] + jnp.sum(p, axis=-1, keepdims=True)
        acc_ref[...] = alpha * acc_ref[...] + jnp.dot(p.astype(jnp.bfloat16), v_blk,
                                                      preferred_element_type=jnp.float32)
        m_ref[...] = m_new

    for j in range(n_blocks - 1):
        @pl.when(j < own)
        def _(j=j):
            attend(kbf_ref[j * blk:(j + 1) * blk, :], vbf_ref[j * blk:(j + 1) * blk, :],
                   sel[:, j:j + 1] > 0.5)

    start = pl.multiple_of(own * blk, blk)
    q_pos = lax.broadcasted_iota(jnp.int32, (rows, blk), 0) & (blk - 1)
    k_pos = lax.broadcasted_iota(jnp.int32, (rows, blk), 1)
    causal = k_pos <= q_pos
    attend(kbf_ref[pl.ds(start, blk), :], vbf_ref[pl.ds(start, blk), :], causal)

    out = acc_ref[...] / l_ref[...]
    for g in range(GROUP):
        o_ref[:, g * HEAD_DIM:(g + 1) * HEAD_DIM] = out[g * blk:(g + 1) * blk, :].astype(o_ref.dtype)


def moba_prompt_attention(q, k, v, *, n_batch, seq):
    rows, d = q.shape
    n_kv = k.shape[1] // HEAD_DIM
    n_blocks = seq // MOBA_BLOCK
    gw = GROUP * HEAD_DIM
    q_spec = pl.BlockSpec((MOBA_BLOCK, gw), lambda b, h, i: (b * n_blocks + i, h))
    kv_spec = pl.BlockSpec((seq, HEAD_DIM), lambda b, h, i: (b, h))
    return pl.pallas_call(
        functools.partial(_moba_prompt_kernel, n_blocks=n_blocks),
        grid=(n_batch, n_kv, n_blocks),
        in_specs=[q_spec, kv_spec, kv_spec],
        out_specs=q_spec,
        out_shape=jax.ShapeDtypeStruct((rows, d), jnp.bfloat16),
        scratch_shapes=[pltpu.VMEM((seq, HEAD_DIM), jnp.bfloat16),
                        pltpu.VMEM((seq, HEAD_DIM), jnp.bfloat16),
                        pltpu.VMEM((n_blocks, HEAD_DIM), jnp.float32),
                        pltpu.VMEM((GROUP * MOBA_BLOCK, 1), jnp.float32),
                        pltpu.VMEM((GROUP * MOBA_BLOCK, 1), jnp.float32),
                        pltpu.VMEM((GROUP * MOBA_BLOCK, HEAD_DIM), jnp.float32)],
        compiler_params=_cparams(("parallel", "parallel", "arbitrary")),
        name="moba_prompt",
    )(q, k, v)


PAGES_PER_STEP = 16


def _page_mean_kernel(pt_ref, *refs, pages_per_block):
    del pt_ref
    page_refs, o_ref = refs[:-1], refs[-1]
    n_out = len(page_refs) // pages_per_block
    page_rows = page_refs[0].shape[0]
    for n in range(n_out):
        total = jnp.sum(page_refs[n * pages_per_block][...], axis=0, keepdims=True)
        for r in range(1, pages_per_block):
            total = total + jnp.sum(page_refs[n * pages_per_block + r][...], axis=0, keepdims=True)
        o_ref[n:n + 1, :] = total / (pages_per_block * page_rows)


def paged_block_means(cache_k3, page_table):
    _, page, width = cache_k3.shape
    n_seq, n_pages = page_table.shape
    pages_per_block = MOBA_BLOCK // page
    blocks_per_step = PAGES_PER_STEP // pages_per_block
    n_steps = n_pages // PAGES_PER_STEP
    in_specs = [pl.BlockSpec((None, page, width),
                             lambda b, s, pt, r=r: (pt[b, s * PAGES_PER_STEP + r], 0, 0))
                for r in range(PAGES_PER_STEP)]
    return pl.pallas_call(
        functools.partial(_page_mean_kernel, pages_per_block=pages_per_block),
        grid_spec=pltpu.PrefetchScalarGridSpec(
            num_scalar_prefetch=1,
            grid=(n_seq, n_steps),
            in_specs=in_specs,
            out_specs=pl.BlockSpec((None, blocks_per_step, width), lambda b, s, pt: (b, s, 0))),
        out_shape=jax.ShapeDtypeStruct((n_seq, n_pages // pages_per_block, width), jnp.float32),
        compiler_params=_cparams(("parallel", "arbitrary")),
        name="page_means",
    )(page_table, *([cache_k3] * PAGES_PER_STEP))


def _sample_topk_kernel(q_ref, kmean_ref, sel_ref, *, n_kv):
    q = q_ref[...]
    n_heads = q.shape[0]
    n_blocks = kmean_ref.shape[0]
    head_kv = lax.broadcasted_iota(jnp.int32, (n_heads, n_blocks), 0) // GROUP
    gate = jnp.zeros((n_heads, n_blocks), jnp.float32)
    for kv in range(n_kv):
        g = lax.dot_general(q, kmean_ref[:, kv * HEAD_DIM:(kv + 1) * HEAD_DIM], (((1,), (1,)), ((), ())),
                            precision=lax.Precision.HIGHEST, preferred_element_type=jnp.float32)
        gate = jnp.where(head_kv == kv, g, gate)
    blk = lax.broadcasted_iota(jnp.int32, gate.shape, 1).astype(jnp.float32)
    lane = lax.broadcasted_iota(jnp.int32, sel_ref.shape, 1)
    sel = jnp.zeros(sel_ref.shape, jnp.float32)
    for s in range(MOBA_TOP_K):
        best = jnp.max(gate, axis=-1, keepdims=True)
        idx = jnp.min(jnp.where(gate == best, blk, float(n_blocks)), axis=-1, keepdims=True)
        sel = jnp.where(lane == s, idx, sel)
        gate = jnp.where(blk == idx, -jnp.inf, gate)
    sel_ref[...] = sel.astype(jnp.int32)


def sample_topk(q_s, kmean):
    n_seq, n_heads, _ = q_s.shape
    _, n_blocks, width = kmean.shape
    return pl.pallas_call(
        functools.partial(_sample_topk_kernel, n_kv=width // HEAD_DIM),
        grid=(n_seq,),
        in_specs=[pl.BlockSpec((None, n_heads, HEAD_DIM), lambda b: (b, 0, 0)),
                  pl.BlockSpec((None, n_blocks, width), lambda b: (b, 0, 0))],
        out_specs=pl.BlockSpec((None, n_heads, LANES), lambda b: (b, 0, 0)),
        out_shape=jax.ShapeDtypeStruct((n_seq, n_heads, LANES), jnp.int32),
        compiler_params=_cparams(("parallel",)),
        name="sample_topk",
    )(q_s, kmean)


def _sample_attend_kernel(pt_ref, sel_ref, q_ref, knew_ref, vnew_ref, *refs, n_pages_sel):
    del pt_ref, sel_ref
    k_refs = refs[:n_pages_sel]
    v_refs = refs[n_pages_sel:2 * n_pages_sel]
    o_ref = refs[2 * n_pages_sel]
    q_scaled = q_ref[...] * (HEAD_DIM ** -0.5)
    qs = q_scaled.astype(jnp.bfloat16)
    qs8 = jnp.broadcast_to(q_scaled, (8, HEAD_DIM)).astype(jnp.bfloat16)
    scores = [lax.dot_general(qs8, k_ref[...].astype(jnp.bfloat16), (((1,), (1,)), ((), ())),
                              preferred_element_type=jnp.float32) for k_ref in k_refs]
    knew = knew_ref[...].astype(jnp.bfloat16).astype(jnp.float32)
    s_own = jnp.sum(qs.astype(jnp.float32) * knew, axis=-1, keepdims=True)
    m = s_own
    for s in scores:
        m = jnp.maximum(m, jnp.max(s[0:1, :], axis=-1, keepdims=True))
    p_own = jnp.exp(s_own - m)
    denom = p_own
    ps = []
    for s in scores:
        p = jnp.exp(s - m)
        ps.append(p)
        denom = denom + jnp.sum(p[0:1, :], axis=-1, keepdims=True)
    inv = 1.0 / denom
    vnew = vnew_ref[...].astype(jnp.bfloat16).astype(jnp.float32)
    out = (p_own * inv).astype(jnp.bfloat16).astype(jnp.float32) * vnew
    for p, v_ref in zip(ps, v_refs):
        pv = jnp.dot((p * inv).astype(jnp.bfloat16), v_ref[...].astype(jnp.bfloat16),
                     preferred_element_type=jnp.float32)
        out = out + pv[0:1, :]
    o_ref[...] = out


def sample_attend(q_s4, k_new4, v_new4, cache_k3, cache_v3, page_table, sel_flat):
    n_seq, n_heads, _, _ = q_s4.shape
    page = cache_k3.shape[1]
    pages_per_block = MOBA_BLOCK // page
    n_pages_sel = MOBA_TOP_K * pages_per_block

    def page_spec(s, r):
        def index(b, h, pt, sel):
            blk = sel[(b * n_heads + h) * MOBA_TOP_K + s]
            return (pt[b, blk * pages_per_block + r], 0, h // GROUP)
        return pl.BlockSpec((None, page, HEAD_DIM), index)

    page_specs = [page_spec(s, r) for s in range(MOBA_TOP_K) for r in range(pages_per_block)]
    head_spec = pl.BlockSpec((None, None, 1, HEAD_DIM), lambda b, h, pt, sel: (b, h, 0, 0))
    kv_spec = pl.BlockSpec((None, None, 1, HEAD_DIM), lambda b, h, pt, sel: (b, h // GROUP, 0, 0))
    return pl.pallas_call(
        functools.partial(_sample_attend_kernel, n_pages_sel=n_pages_sel),
        grid_spec=pltpu.PrefetchScalarGridSpec(
            num_scalar_prefetch=2,
            grid=(n_seq, n_heads),
            in_specs=[head_spec, kv_spec, kv_spec] + page_specs + page_specs,
            out_specs=head_spec),
        out_shape=jax.ShapeDtypeStruct(q_s4.shape, jnp.float32),
        compiler_params=_cparams(("parallel", "parallel")),
        name="sample_attend",
    )(page_table, sel_flat, q_s4, k_new4, v_new4, *([cache_k3] * n_pages_sel), *([cache_v3] * n_pages_sel))


def _write_sample_rows_kernel(dst_in_ref, src_ref, dst_ref):
    del dst_in_ref
    dst_ref[...] = src_ref[...].astype(dst_ref.dtype)


def write_sample_rows(dst, src, *, row_block):
    d = dst.shape[1]
    return pl.pallas_call(
        _write_sample_rows_kernel,
        grid=(1,),
        in_specs=[pl.BlockSpec(memory_space=pl.ANY), pl.BlockSpec((ROW_PAD, d), lambda i: (0, 0))],
        out_specs=pl.BlockSpec((ROW_PAD, d), lambda i: (row_block, 0)),
        out_shape=jax.ShapeDtypeStruct(dst.shape, dst.dtype),
        input_output_aliases={0: 0},
        compiler_params=_cparams(("arbitrary",)),
        name="write_sample_rows",
    )(dst, src)


def _rope_tables(positions):
    inv_freq = ROPE_THETA ** (-jnp.arange(ROT_HALF, dtype=jnp.float32) * 2.0 / ROT_DIM)
    ang = positions.astype(jnp.float32)[:, None] * inv_freq[None, :]
    cos, sin = jnp.cos(ang), jnp.sin(ang)
    rest = HEAD_DIM - ROT_DIM
    ones = jnp.ones((positions.shape[0], rest), jnp.float32)
    zeros = jnp.zeros((positions.shape[0], rest), jnp.float32)
    return (jnp.concatenate([cos, cos, ones], axis=1), jnp.concatenate([-sin, sin, zeros], axis=1))


def kernel(x_prompt, x_sample, state_conv, cache_k, cache_v, page_table, a_norm, a_w_in, a_conv_w, a_w_out,
           kv_norm, w_k, w_v, b_norm, w_q, w_o, ffn_norm, w_gate, w_up, w_down, final_norm):
    n_batch, seq, d_model = x_prompt.shape
    n_sample = x_sample.shape[0]
    assert x_sample.shape[1] == 1 and n_sample <= ROW_PAD
    assert a_w_in.shape[0] == 1 and w_q.shape[0] == 1, "one short-conv layer then one MoBA layer"
    n_pool, page, n_kv, _ = cache_k.shape
    n_heads = d_model // HEAD_DIM
    past_len = page_table.shape[1] * page
    assert past_len % MOBA_BLOCK == 0 and past_len // MOBA_BLOCK >= MOBA_TOP_K
    rows_p = n_batch * seq
    rows = rows_p + ROW_PAD
    bf16 = jnp.bfloat16

    tr, tm, tm_res, tm_ffn = (_row_tile(rows, t) for t in (320, 1040, 832, 1664))

    x = jnp.concatenate([x_prompt.reshape(rows_p, d_model), x_sample.reshape(n_sample, d_model),
                         jnp.zeros((ROW_PAD - n_sample, d_model), x_prompt.dtype)], axis=0)
    positions = jnp.concatenate([jnp.tile(jnp.arange(seq, dtype=jnp.int32), n_batch),
                                 jnp.full((n_sample,), past_len, jnp.int32),
                                 jnp.zeros((ROW_PAD - n_sample,), jnp.int32)])
    cos, sin_signed = _rope_tables(positions)

    def ffn(h, layer):
        hn, = rmsnorm(h, ffn_norm[layer:layer + 1], [bf16], rows_per_step=tr)
        act = matmul(hn, [w_gate[layer].astype(bf16), w_up[layer].astype(bf16)], tm=tm_ffn, tn=256,
                     mode="swiglu", out_dtype=bf16, name="ffn_gate_up")
        d_ff = act.shape[1]
        return matmul(act, [w_down[layer].astype(bf16)], tm=tm, tn=512, tk=d_ff // 2, mode="residual",
                      extra=(h,), name="ffn_down")

    xn, = rmsnorm(x, a_norm, [bf16], rows_per_step=tr)
    bcx = matmul(xn, [a_w_in[0].astype(bf16)], tm=tm, tn=1024, name="conv_in")
    g, conv_prompt, conv_sample = conv_gate(bcx, a_conv_w[0], state_conv[0], n_batch=n_batch, seq=seq,
                                            n_sample=n_sample, tc=256)
    h = matmul(g, [a_w_out[0].astype(bf16)], tm=tm_res, tn=1024, mode="residual", extra=(x,), name="conv_out")
    h = ffn(h, 0)

    kv_in, q_in = rmsnorm(h, jnp.stack([kv_norm, b_norm[0]]), [bf16, bf16], rows_per_step=tr)
    k = matmul(kv_in, [w_k.astype(bf16)], tm=tm, tn=1024, mode="rotary", extra=(cos, sin_signed), name="k_proj")
    v = matmul(kv_in, [w_v.astype(bf16)], tm=tm, tn=1024, name="v_proj")
    q = matmul(q_in, [w_q[0].astype(bf16)], tm=tm, tn=1024, mode="rotary", extra=(cos, sin_signed), name="q_proj")

    attn = moba_prompt_attention(q, k, v, n_batch=n_batch, seq=seq)

    kvw = n_kv * HEAD_DIM
    cache_k3 = cache_k.reshape(n_pool, page, kvw)
    cache_v3 = cache_v.reshape(n_pool, page, kvw)
    kmean = paged_block_means(cache_k3, page_table)
    q_s = q[rows_p:rows_p + n_sample].reshape(n_sample, n_heads, HEAD_DIM)
    sel = sample_topk(q_s, kmean)[:, :, :MOBA_TOP_K].reshape(-1)
    k_s = k[rows_p:rows_p + n_sample].reshape(n_sample, n_kv, 1, HEAD_DIM)
    v_s = v[rows_p:rows_p + n_sample].reshape(n_sample, n_kv, 1, HEAD_DIM)
    attn_s = sample_attend(q_s.reshape(n_sample, n_heads, 1, HEAD_DIM), k_s, v_s, cache_k3, cache_v3,
                           page_table, sel)
    attn_s = jnp.concatenate([attn_s.reshape(n_sample, d_model),
                              jnp.zeros((ROW_PAD - n_sample, d_model), jnp.float32)], axis=0)
    attn = write_sample_rows(attn, attn_s, row_block=rows_p // ROW_PAD)

    h = matmul(attn, [w_o[0].astype(bf16)], tm=tm_res, tn=1024, mode="residual", extra=(h,), name="attn_out")
    h = ffn(h, 1)
    y, = rmsnorm(h, final_norm[None, :], [jnp.float32], rows_per_step=tr)

    y_prompt = y[:rows_p].reshape(n_batch, seq, d_model)
    y_sample = y[rows_p:rows_p + n_sample].reshape(n_sample, 1, d_model)
    k_prompt = k[:rows_p].reshape(n_batch, seq, n_kv, HEAD_DIM)
    v_prompt = v[:rows_p].reshape(n_batch, seq, n_kv, HEAD_DIM)
    return (y_prompt, y_sample, conv_prompt[None], conv_sample[None], k_prompt, v_prompt,
            k_s.reshape(n_sample, 1, n_kv, HEAD_DIM), v_s.reshape(n_sample, 1, n_kv, HEAD_DIM))
```

```python
import functools

import jax
import jax.numpy as jnp
from jax import lax
from jax.experimental import pallas as pl
from jax.experimental.pallas import tpu as pltpu

HEAD_DIM = 128
GROUP = 4
ROT_DIM = HEAD_DIM // 4
ROT_HALF = ROT_DIM // 2
ROPE_THETA = 500000.0
MOBA_BLOCK = 256
MOBA_TOP_K = 3
CONV_WIDTH = 3
RMS_EPS = 1e-6
NEG_INF = -1e30

LANES = 128
BF16_SUBLANES = 16
VMEM_LIMIT_BYTES = 58 * 1024 * 1024

ROW_PAD = 128
BF16 = jnp.bfloat16


def _row_tile(rows, target):
    return max(t for t in range(BF16_SUBLANES, target + 1, BF16_SUBLANES) if rows % t == 0)


def _cparams(semantics):
    return pltpu.CompilerParams(dimension_semantics=semantics, vmem_limit_bytes=VMEM_LIMIT_BYTES)


def _rmsnorm_kernel(x_ref, g_ref, *o_refs):
    x = x_ref[...]
    xn = x * lax.rsqrt(jnp.mean(x * x, axis=-1, keepdims=True) + RMS_EPS)
    for i, o_ref in enumerate(o_refs):
        o_ref[...] = (xn * g_ref[i:i + 1, :]).astype(o_ref.dtype)


def rmsnorm(x, gains, out_dtypes, *, rows_per_step, row_block0=0, n_row_blocks=None):
    rows, d = x.shape
    n_out = gains.shape[0]
    if n_row_blocks is None:
        n_row_blocks = rows // rows_per_step
    out_spec = pl.BlockSpec((rows_per_step, d), lambda r: (r, 0))
    return pl.pallas_call(
        _rmsnorm_kernel,
        grid=(n_row_blocks,),
        in_specs=[pl.BlockSpec((rows_per_step, d), lambda r: (r + row_block0, 0)),
                  pl.BlockSpec((n_out, d), lambda r: (0, 0))],
        out_specs=[out_spec] * n_out,
        out_shape=[jax.ShapeDtypeStruct((n_row_blocks * rows_per_step, d), dt) for dt in out_dtypes],
        compiler_params=_cparams(("parallel",)),
        name="rmsnorm",
    )(x, gains)


def _rotary_epilogue(y, cos, sin_signed):
    lane = lax.broadcasted_iota(jnp.int32, (y.shape[0], HEAD_DIM), 1)
    first_half = lane < ROT_HALF
    outs = []
    for h in range(y.shape[1] // HEAD_DIM):
        yh = y[:, h * HEAD_DIM:(h + 1) * HEAD_DIM]
        from_above = pltpu.roll(yh, HEAD_DIM - ROT_HALF, axis=1)
        from_below = pltpu.roll(yh, ROT_HALF, axis=1)
        partner = jnp.where(first_half, from_above, from_below)
        outs.append(yh * cos + partner * sin_signed)
    return outs[0] if len(outs) == 1 else jnp.concatenate(outs, axis=1)


def _matmul_kernel(*refs, n_w, n_k, mode):
    x_ref = refs[0]
    w_refs = refs[1:1 + n_w]
    pos = 1 + n_w
    extra = ()
    if mode == "residual":
        extra = refs[pos:pos + 1]
        pos += 1
    elif mode in ("rotary", "rotary_plain"):
        extra = refs[pos:pos + 2]
        pos += 2
    n_out = 2 if mode == "rotary_plain" else 1
    o_refs = refs[pos:pos + n_out]
    acc_refs = refs[pos + n_out:]

    x = x_ref[...]
    parts = [jnp.dot(x, w_ref[...].astype(x.dtype), preferred_element_type=jnp.float32) for w_ref in w_refs]

    def finish(vals):
        if mode == "swiglu":
            g, u = vals
            ys = [(g * jax.nn.sigmoid(g)) * u]
        elif mode == "residual":
            ys = [extra[0][...] + vals[0]]
        elif mode == "rotary":
            ys = [_rotary_epilogue(vals[0], extra[0][...], extra[1][...])]
        elif mode == "rotary_plain":
            ys = [_rotary_epilogue(vals[0], extra[0][...], extra[1][...]), vals[1]]
        else:
            ys = vals
        for o_ref, y in zip(o_refs, ys):
            o_ref[...] = y.astype(o_ref.dtype)

    if n_k == 1:
        finish(parts)
        return

    k = pl.program_id(2)

    @pl.when(k == 0)
    def _():
        for acc_ref, p in zip(acc_refs, parts):
            acc_ref[...] = p

    if n_k > 2:
        @pl.when(jnp.logical_and(k > 0, k < n_k - 1))
        def _():
            for acc_ref, p in zip(acc_refs, parts):
                acc_ref[...] += p

    @pl.when(k == n_k - 1)
    def _():
        finish([acc_ref[...] + p for acc_ref, p in zip(acc_refs, parts)])


def matmul(x, ws, *, tm, tn, tk=None, mode="plain", extra=(), out_dtype=jnp.float32, name="matmul",
           row_block0=0, n_row_blocks=None):
    m, k_dim = x.shape
    n = ws[0].shape[1]
    tn = min(tn, n)
    tk = k_dim if tk is None else tk
    n_k = k_dim // tk
    if n_row_blocks is None:
        n_row_blocks = m // tm
    assert n % tn == 0 and k_dim % tk == 0 and (row_block0 + n_row_blocks) * tm <= m
    in_specs = [pl.BlockSpec((tm, tk), lambda i, j, k: (i + row_block0, k))]
    in_specs += [pl.BlockSpec((tk, tn), lambda i, j, k: (k, j)) for _ in ws]
    if mode == "residual":
        in_specs.append(pl.BlockSpec((tm, tn), lambda i, j, k: (i, j)))
    elif mode in ("rotary", "rotary_plain"):
        in_specs += [pl.BlockSpec((tm, HEAD_DIM), lambda i, j, k: (i, 0))] * 2
    n_out = 2 if mode == "rotary_plain" else 1
    out_spec = pl.BlockSpec((tm, tn), lambda i, j, k: (i, j))
    out_shape = jax.ShapeDtypeStruct((n_row_blocks * tm, n), out_dtype)
    scratch = [pltpu.VMEM((tm, tn), jnp.float32) for _ in ws] if n_k > 1 else []
    outs = pl.pallas_call(
        functools.partial(_matmul_kernel, n_w=len(ws), n_k=n_k, mode=mode),
        grid=(n_row_blocks, n // tn, n_k),
        in_specs=in_specs,
        out_specs=[out_spec] * n_out,
        out_shape=[out_shape] * n_out,
        scratch_shapes=scratch,
        compiler_params=_cparams(("parallel", "parallel", "arbitrary")),
        name=name,
    )(x, *ws, *extra)
    return outs if n_out > 1 else outs[0]


def _conv_gate_kernel(b_ref, c_ref, xin_ref, bs_ref, cs_ref, xs_ref, st0_ref, st1_ref, w_ref,
                      g_ref, state_ref, us_ref, *, n_batch):
    step = pl.program_id(1)

    @pl.when(step < n_batch)
    def _():
        u = c_ref[...] * xin_ref[...]
        t = lax.broadcasted_iota(jnp.int32, u.shape, 0)
        u_prev1 = jnp.where(t >= 1, pltpu.roll(u, 1, axis=0), 0.0)
        u_prev2 = jnp.where(t >= 2, pltpu.roll(u, 2, axis=0), 0.0)
        conv = w_ref[0:1, :] * u_prev2
        conv = conv + w_ref[1:2, :] * u_prev1
        conv = conv + w_ref[2:3, :] * u
        g_ref[...] = (b_ref[...] * conv).astype(g_ref.dtype)
        tail = pl.ds(u.shape[0] - (CONV_WIDTH - 1), CONV_WIDTH - 1)
        state_ref[...] = c_ref[tail, :] * xin_ref[tail, :]

    @pl.when(step == n_batch)
    def _():
        u = cs_ref[...] * xs_ref[...]
        conv = w_ref[0:1, :] * st0_ref[...]
        conv = conv + w_ref[1:2, :] * st1_ref[...]
        conv = conv + w_ref[2:3, :] * u
        g_ref[0:ROW_PAD, :] = (bs_ref[...] * conv).astype(g_ref.dtype)
        us_ref[...] = u


def conv_gate(bcx, conv_w, state_sample, *, n_batch, seq, n_sample, tc):
    rows, d3 = bcx.shape
    d = d3 // 3
    n_c = d // tc
    pad = jnp.zeros((ROW_PAD - n_sample, d), jnp.float32)
    st0 = jnp.concatenate([state_sample[:, 0, :], pad], axis=0)
    st1 = jnp.concatenate([state_sample[:, 1, :], pad], axis=0)
    sample_blk = (n_batch * seq) // ROW_PAD
    last = n_batch - 1

    def seq_spec(col0):
        return pl.BlockSpec((seq, tc), lambda j, s: (jnp.minimum(s, last), col0 + j))

    def sample_spec(col0):
        return pl.BlockSpec((ROW_PAD, tc), lambda j, s: (sample_blk, col0 + j))

    const_spec = pl.BlockSpec((ROW_PAD, tc), lambda j, s: (0, j))
    g, st_prompt, u_sample = pl.pallas_call(
        functools.partial(_conv_gate_kernel, n_batch=n_batch),
        grid=(n_c, n_batch + 1),
        in_specs=[seq_spec(0), seq_spec(n_c), seq_spec(2 * n_c),
                  sample_spec(0), sample_spec(n_c), sample_spec(2 * n_c),
                  const_spec, const_spec, pl.BlockSpec((CONV_WIDTH, tc), lambda j, s: (0, j))],
        out_specs=[pl.BlockSpec((seq, tc), lambda j, s: (s, j)),
                   pl.BlockSpec((None, CONV_WIDTH - 1, tc), lambda j, s: (jnp.minimum(s, last), 0, j)),
                   const_spec],
        out_shape=[jax.ShapeDtypeStruct((rows, d), BF16),
                   jax.ShapeDtypeStruct((n_batch, CONV_WIDTH - 1, d), jnp.float32),
                   jax.ShapeDtypeStruct((ROW_PAD, d), jnp.float32)],
        compiler_params=_cparams(("parallel", "arbitrary")),
        name="conv_gate",
    )(bcx, bcx, bcx, bcx, bcx, bcx, st0, st1, conv_w)
    st_sample = jnp.stack([state_sample[:, 1, :], u_sample[:n_sample]], axis=1)
    return g, st_prompt, st_sample


def _moba_kernel(q_ref, k_ref, v_ref, sample_ref, o_ref, *scratch, n_blocks, n_prompt_tiles):
    tile = pl.program_id(1)

    @pl.when(tile < n_prompt_tiles)
    def _():
        _moba_prompt_tile(q_ref, k_ref, v_ref, o_ref, *scratch, own=tile % n_blocks, n_blocks=n_blocks)

    @pl.when(tile == n_prompt_tiles)
    def _():
        o_ref[0:ROW_PAD, :] = sample_ref[...].astype(o_ref.dtype)


def _moba_prompt_tile(q_ref, k_ref, v_ref, o_ref, kbf_ref, vt_ref, kmean_ref, qs_ref, sel_ref,
                      s_ref, m_ref, l_ref, acc_ref, *, own, n_blocks):
    blk = MOBA_BLOCK
    nt_dims = (((1,), (1,)), ((), ()))

    @pl.when(own == 0)
    def _():
        kbf_ref[...] = k_ref[...].astype(BF16)
        for n in range(n_blocks):
            rows = slice(n * blk, (n + 1) * blk)
            kmean_ref[n:n + 1, :] = jnp.mean(k_ref[rows, :], axis=0, keepdims=True)
            vt_ref[n] = jnp.transpose(v_ref[rows, :]).astype(BF16)

    for g in range(GROUP):
        q = q_ref[:, g * HEAD_DIM:(g + 1) * HEAD_DIM]
        gate_t = lax.dot_general(kmean_ref[...], q, nt_dims, precision=lax.Precision.HIGHEST,
                                 preferred_element_type=jnp.float32)
        n_iota = lax.broadcasted_iota(jnp.int32, gate_t.shape, 0)
        beaten_by = jnp.zeros(gate_t.shape, jnp.int32)
        for n in range(n_blocks):
            row = gate_t[n:n + 1, :]
            beats = jnp.logical_or(row > gate_t, jnp.logical_and(row == gate_t, n < n_iota))
            beaten_by = beaten_by + jnp.where(beats, 1, 0) * (n < own).astype(jnp.int32)
        sel_t = jnp.logical_and(beaten_by < MOBA_TOP_K, n_iota < own)
        sel_ref[g] = jnp.where(sel_t, 1.0, 0.0)
        qs_ref[g] = (q * (HEAD_DIM ** -0.5)).astype(BF16)
    m_ref[...] = jnp.full(m_ref.shape, -jnp.inf, jnp.float32)
    l_ref[...] = jnp.zeros(l_ref.shape, jnp.float32)
    acc_ref[...] = jnp.zeros(acc_ref.shape, jnp.float32)

    def score_block(j, keep_of_head):
        k_blk = kbf_ref[pl.ds(pl.multiple_of(j * blk, blk), blk), :]
        for g in range(GROUP):
            s_t = lax.dot_general(k_blk, qs_ref[g], nt_dims, preferred_element_type=jnp.float32)
            s_t = jnp.where(keep_of_head(g), s_t, NEG_INF)
            s_ref[g, j] = s_t
            m_ref[g:g + 1, :] = jnp.maximum(m_ref[g:g + 1, :], jnp.max(s_t, axis=0, keepdims=True))

    def score_past(j, carry):
        score_block(j, lambda g: sel_ref[g, pl.ds(j, 1), :] > 0.5)
        return carry

    lax.fori_loop(0, own, score_past, 0)
    k_pos = lax.broadcasted_iota(jnp.int32, (blk, blk), 0)
    q_pos = lax.broadcasted_iota(jnp.int32, (blk, blk), 1)
    causal = k_pos <= q_pos
    score_block(own, lambda g: causal)

    def weigh_block(j, carry):
        for g in range(GROUP):
            p_t = jnp.exp(s_ref[g, j] - m_ref[g:g + 1, :])
            l_ref[g:g + 1, :] += jnp.sum(p_t, axis=0, keepdims=True)
            acc_ref[g] += jnp.dot(vt_ref[j], p_t.astype(BF16), preferred_element_type=jnp.float32)
        return carry

    lax.fori_loop(0, own + 1, weigh_block, 0)

    for g in range(GROUP):
        out_t = acc_ref[g] / l_ref[g:g + 1, :]
        o_ref[:, g * HEAD_DIM:(g + 1) * HEAD_DIM] = jnp.transpose(out_t).astype(o_ref.dtype)


def moba_attention(q, k, v, attn_sample, *, n_batch, seq):
    rows, d = q.shape
    n_kv = k.shape[1] // HEAD_DIM
    n_blocks = seq // MOBA_BLOCK
    n_prompt_tiles = n_batch * n_blocks
    gw = GROUP * HEAD_DIM
    q_spec = pl.BlockSpec((MOBA_BLOCK, gw), lambda h, t: (jnp.minimum(t, n_prompt_tiles - 1), h))
    kv_spec = pl.BlockSpec((seq, HEAD_DIM), lambda h, t: (jnp.minimum(t // n_blocks, n_batch - 1), h))
    return pl.pallas_call(
        functools.partial(_moba_kernel, n_blocks=n_blocks, n_prompt_tiles=n_prompt_tiles),
        grid=(n_kv, n_prompt_tiles + 1),
        in_specs=[q_spec, kv_spec, kv_spec, pl.BlockSpec((ROW_PAD, gw), lambda h, t: (0, h))],
        out_specs=pl.BlockSpec((MOBA_BLOCK, gw), lambda h, t: (t, h)),
        out_shape=jax.ShapeDtypeStruct((rows, d), BF16),
        scratch_shapes=[pltpu.VMEM((seq, HEAD_DIM), BF16),
                        pltpu.VMEM((n_blocks, HEAD_DIM, MOBA_BLOCK), BF16),
                        pltpu.VMEM((n_blocks, HEAD_DIM), jnp.float32),
                        pltpu.VMEM((GROUP, MOBA_BLOCK, HEAD_DIM), BF16),
                        pltpu.VMEM((GROUP, n_blocks, MOBA_BLOCK), jnp.float32),
                        pltpu.VMEM((GROUP, n_blocks, MOBA_BLOCK, MOBA_BLOCK), jnp.float32),
                        pltpu.VMEM((GROUP, MOBA_BLOCK), jnp.float32),
                        pltpu.VMEM((GROUP, MOBA_BLOCK), jnp.float32),
                        pltpu.VMEM((GROUP, HEAD_DIM, MOBA_BLOCK), jnp.float32)],
        compiler_params=_cparams(("parallel", "arbitrary")),
        name="moba_attention",
    )(q, k, v, attn_sample)


PAGES_PER_STEP = 16


def _page_mean_kernel(pt_ref, *refs, pages_per_block):
    del pt_ref
    page_refs, o_ref = refs[:-1], refs[-1]
    n_out = len(page_refs) // pages_per_block
    page_rows = page_refs[0].shape[0]
    for n in range(n_out):
        total = jnp.sum(page_refs[n * pages_per_block][...], axis=0)
        for r in range(1, pages_per_block):
            total = total + jnp.sum(page_refs[n * pages_per_block + r][...], axis=0)
        o_ref[n] = total / (pages_per_block * page_rows)


def paged_block_means(cache_k, page_table):
    _, page, n_kv, _ = cache_k.shape
    n_seq, n_pages = page_table.shape
    pages_per_block = MOBA_BLOCK // page
    blocks_per_step = PAGES_PER_STEP // pages_per_block
    n_steps = n_pages // PAGES_PER_STEP
    in_specs = [pl.BlockSpec((None, page, n_kv, HEAD_DIM),
                             lambda b, s, pt, r=r: (pt[b, s * PAGES_PER_STEP + r], 0, 0, 0))
                for r in range(PAGES_PER_STEP)]
    return pl.pallas_call(
        functools.partial(_page_mean_kernel, pages_per_block=pages_per_block),
        grid_spec=pltpu.PrefetchScalarGridSpec(
            num_scalar_prefetch=1,
            grid=(n_seq, n_steps),
            in_specs=in_specs,
            out_specs=pl.BlockSpec((None, blocks_per_step, n_kv, HEAD_DIM), lambda b, s, pt: (b, s, 0, 0))),
        out_shape=jax.ShapeDtypeStruct((n_seq, n_pages // pages_per_block, n_kv, HEAD_DIM), jnp.float32),
        compiler_params=_cparams(("parallel", "arbitrary")),
        name="page_means",
    )(page_table, *([cache_k] * PAGES_PER_STEP))


def _sample_topk_kernel(q_ref, kmean_ref, sel_ref):
    q = q_ref[...]
    n_heads = q.shape[0]
    n_kv, n_blocks, _ = kmean_ref.shape
    head_kv = lax.broadcasted_iota(jnp.int32, (n_heads, n_blocks), 0) // GROUP
    gate = jnp.zeros((n_heads, n_blocks), jnp.float32)
    for kv in range(n_kv):
        g = lax.dot_general(q, kmean_ref[kv], (((1,), (1,)), ((), ())),
                            precision=lax.Precision.HIGHEST, preferred_element_type=jnp.float32)
        gate = jnp.where(head_kv == kv, g, gate)
    blk = lax.broadcasted_iota(jnp.int32, gate.shape, 1).astype(jnp.float32)
    lane = lax.broadcasted_iota(jnp.int32, sel_ref.shape, 1)
    sel = jnp.zeros(sel_ref.shape, jnp.float32)
    for s in range(MOBA_TOP_K):
        best = jnp.max(gate, axis=-1, keepdims=True)
        idx = jnp.min(jnp.where(gate == best, blk, float(n_blocks)), axis=-1, keepdims=True)
        sel = jnp.where(lane == s, idx, sel)
        gate = jnp.where(blk == idx, -jnp.inf, gate)
    sel_ref[...] = sel.astype(jnp.int32)


def sample_topk(q_s, kmean_t):
    n_seq, n_heads, _ = q_s.shape
    _, n_kv, n_blocks, _ = kmean_t.shape
    return pl.pallas_call(
        _sample_topk_kernel,
        grid=(n_seq,),
        in_specs=[pl.BlockSpec((None, n_heads, HEAD_DIM), lambda b: (b, 0, 0)),
                  pl.BlockSpec((None, n_kv, n_blocks, HEAD_DIM), lambda b: (b, 0, 0, 0))],
        out_specs=pl.BlockSpec((None, n_heads, LANES), lambda b: (b, 0, 0)),
        out_shape=jax.ShapeDtypeStruct((n_seq, n_heads, LANES), jnp.int32),
        compiler_params=_cparams(("parallel",)),
        name="sample_topk",
    )(q_s, kmean_t)


def _sample_attend_kernel(pt_ref, sel_ref, q_ref, knew_ref, vnew_ref, ck_hbm, cv_hbm, o_ref,
                          kbuf, vbuf, sems, *, pages_per_block):
    n_seq, n_heads, _, _ = q_ref.shape
    n_pages_sel = kbuf.shape[1]
    n_items = n_seq * n_heads

    def page_copies(item, slot):
        b = item // n_heads
        kv = (item % n_heads) // GROUP
        copies = []
        for idx in range(n_pages_sel):
            blk = sel_ref[item * MOBA_TOP_K + idx // pages_per_block]
            pg = pt_ref[b, blk * pages_per_block + idx % pages_per_block]
            copies.append(pltpu.make_async_copy(ck_hbm.at[pg, :, kv, :], kbuf.at[slot, idx], sems.at[0, slot, idx]))
            copies.append(pltpu.make_async_copy(cv_hbm.at[pg, :, kv, :], vbuf.at[slot, idx], sems.at[1, slot, idx]))
        return copies

    for c in page_copies(0, 0):
        c.start()

    def step(item, carry):
        slot = item % 2

        @pl.when(item + 1 < n_items)
        def _():
            for c in page_copies(item + 1, 1 - slot):
                c.start()

        for c in page_copies(item, slot):
            c.wait()
        b = item // n_heads
        h = item % n_heads
        o_ref[b, h] = _sample_attend_one(q_ref[b, h], knew_ref[b, h // GROUP], vnew_ref[b, h // GROUP],
                                         [kbuf[slot, idx] for idx in range(n_pages_sel)],
                                         [vbuf[slot, idx] for idx in range(n_pages_sel)])
        return carry

    lax.fori_loop(0, n_items, step, 0)


def _sample_attend_one(q, k_new, v_new, k_pages, v_pages):
    q_scaled = q * (HEAD_DIM ** -0.5)
    qs = q_scaled.astype(BF16)
    qs8 = jnp.broadcast_to(q_scaled, (8, HEAD_DIM)).astype(BF16)
    scores = [lax.dot_general(qs8, kp.astype(BF16), (((1,), (1,)), ((), ())),
                              preferred_element_type=jnp.float32) for kp in k_pages]
    knew = k_new.astype(BF16).astype(jnp.float32)
    s_own = jnp.sum(qs.astype(jnp.float32) * knew, axis=-1, keepdims=True)
    m = s_own
    for s in scores:
        m = jnp.maximum(m, jnp.max(s[0:1, :], axis=-1, keepdims=True))
    p_own = jnp.exp(s_own - m)
    denom = p_own
    ps = []
    for s in scores:
        p = jnp.exp(s - m)
        ps.append(p)
        denom = denom + jnp.sum(p[0:1, :], axis=-1, keepdims=True)
    inv = 1.0 / denom
    vnew = v_new.astype(BF16).astype(jnp.float32)
    out = (p_own * inv).astype(BF16).astype(jnp.float32) * vnew
    for p, vp in zip(ps, v_pages):
        pv = jnp.dot((p * inv).astype(BF16), vp.astype(BF16), preferred_element_type=jnp.float32)
        out = out + pv[0:1, :]
    return out


def sample_attend(q_s4, k_new4, v_new4, cache_k, cache_v, page_table, sel_flat):
    page = cache_k.shape[1]
    pages_per_block = MOBA_BLOCK // page
    n_pages_sel = MOBA_TOP_K * pages_per_block

    def whole(a):
        return pl.BlockSpec(a.shape, lambda i, pt, sel: (0,) * a.ndim)

    return pl.pallas_call(
        functools.partial(_sample_attend_kernel, pages_per_block=pages_per_block),
        grid_spec=pltpu.PrefetchScalarGridSpec(
            num_scalar_prefetch=2,
            grid=(1,),
            in_specs=[whole(q_s4), whole(k_new4), whole(v_new4),
                      pl.BlockSpec(memory_space=pl.ANY), pl.BlockSpec(memory_space=pl.ANY)],
            out_specs=whole(q_s4),
            scratch_shapes=[pltpu.VMEM((2, n_pages_sel, page, HEAD_DIM), jnp.float32),
                            pltpu.VMEM((2, n_pages_sel, page, HEAD_DIM), jnp.float32),
                            pltpu.SemaphoreType.DMA((2, 2, n_pages_sel))]),
        out_shape=jax.ShapeDtypeStruct(q_s4.shape, jnp.float32),
        compiler_params=_cparams(("arbitrary",)),
        name="sample_attend",
    )(page_table, sel_flat, q_s4, k_new4, v_new4, cache_k, cache_v)


def _rope_tables(positions):
    inv_freq = ROPE_THETA ** (-jnp.arange(ROT_HALF, dtype=jnp.float32) * 2.0 / ROT_DIM)
    ang = positions.astype(jnp.float32)[:, None] * inv_freq[None, :]
    cos, sin = jnp.cos(ang), jnp.sin(ang)
    rest = HEAD_DIM - ROT_DIM
    ones = jnp.ones((positions.shape[0], rest), jnp.float32)
    zeros = jnp.zeros((positions.shape[0], rest), jnp.float32)
    return (jnp.concatenate([cos, cos, ones], axis=1), jnp.concatenate([-sin, sin, zeros], axis=1))


def kernel(x_prompt, x_sample, state_conv, cache_k, cache_v, page_table, a_norm, a_w_in, a_conv_w, a_w_out,
           kv_norm, w_k, w_v, b_norm, w_q, w_o, ffn_norm, w_gate, w_up, w_down, final_norm):
    n_batch, seq, d_model = x_prompt.shape
    n_sample = x_sample.shape[0]
    assert x_sample.shape[1] == 1 and n_sample <= ROW_PAD
    assert a_w_in.shape[0] == 1 and w_q.shape[0] == 1, "one short-conv layer then one MoBA layer"
    _, page, n_kv, _ = cache_k.shape
    n_heads = d_model // HEAD_DIM
    past_len = page_table.shape[1] * page
    assert past_len % MOBA_BLOCK == 0 and past_len // MOBA_BLOCK >= MOBA_TOP_K
    rows_p = n_batch * seq
    rows = rows_p + ROW_PAD
    sample_blk = rows_p // ROW_PAD

    tr, tm, tm_res, tm_ffn = (_row_tile(rows, t) for t in (320, 1040, 832, 1664))
    tr_p, tm_p = _row_tile(rows_p, 256), _row_tile(rows_p, 1024)

    x = jnp.concatenate([x_prompt.reshape(rows_p, d_model), x_sample.reshape(n_sample, d_model),
                         jnp.zeros((ROW_PAD - n_sample, d_model), x_prompt.dtype)], axis=0)
    pos_p = jnp.tile(jnp.arange(seq, dtype=jnp.int32), n_batch)
    pos_s = jnp.concatenate([jnp.full((n_sample,), past_len, jnp.int32),
                             jnp.zeros((ROW_PAD - n_sample,), jnp.int32)])
    rope_p, rope_s = _rope_tables(pos_p), _rope_tables(pos_s)
    rope_all = tuple(jnp.concatenate([p, s], axis=0) for p, s in zip(rope_p, rope_s))

    def ffn(h, layer):
        hn, = rmsnorm(h, ffn_norm[layer:layer + 1], [BF16], rows_per_step=tr)
        act = matmul(hn, [w_gate[layer], w_up[layer]], tm=tm_ffn, tn=256, mode="swiglu", out_dtype=BF16,
                     name="ffn_gate_up")
        d_ff = act.shape[1]
        return matmul(act, [w_down[layer].astype(BF16)], tm=tm, tn=512, tk=d_ff // 2, mode="residual",
                      extra=(h,), name="ffn_down")

    xn, = rmsnorm(x, a_norm, [BF16], rows_per_step=tr)
    bcx = matmul(xn, [a_w_in[0]], tm=tm, tn=512, name="conv_in")
    g, conv_prompt, conv_sample = conv_gate(bcx, a_conv_w[0], state_conv[0], n_batch=n_batch, seq=seq,
                                            n_sample=n_sample, tc=256)
    h = matmul(g, [a_w_out[0]], tm=tm_res, tn=512, mode="residual", extra=(x,), name="conv_out")
    h = ffn(h, 0)

    kv_in, q_in = rmsnorm(h, jnp.stack([kv_norm, b_norm[0]]), [BF16, BF16], rows_per_step=tr)
    k_p, v_p = matmul(kv_in, [w_k, w_v], tm=tm_p, tn=256, mode="rotary_plain", extra=rope_p, name="kv_proj",
                      n_row_blocks=rows_p // tm_p)
    k_s, v_s = matmul(kv_in, [w_k, w_v], tm=ROW_PAD, tn=256, mode="rotary_plain", extra=rope_s,
                      name="kv_proj_sample", row_block0=sample_blk, n_row_blocks=1)
    q = matmul(q_in, [w_q[0]], tm=tm, tn=512, mode="rotary", extra=rope_all, name="q_proj")

    kmean = paged_block_means(cache_k, page_table)
    q_s = q[rows_p:rows_p + n_sample].reshape(n_sample, n_heads, HEAD_DIM)
    sel = sample_topk(q_s, kmean.transpose(0, 2, 1, 3))[:, :, :MOBA_TOP_K].reshape(-1)
    k_s = k_s[:n_sample].reshape(n_sample, n_kv, 1, HEAD_DIM)
    v_s = v_s[:n_sample].reshape(n_sample, n_kv, 1, HEAD_DIM)
    attn_s = sample_attend(q_s.reshape(n_sample, n_heads, 1, HEAD_DIM), k_s, v_s, cache_k, cache_v,
                           page_table, sel)
    attn_s = jnp.concatenate([attn_s.reshape(n_sample, d_model),
                              jnp.zeros((ROW_PAD - n_sample, d_model), jnp.float32)], axis=0)
    attn = moba_attention(q, k_p, v_p, attn_s, n_batch=n_batch, seq=seq)

    h = matmul(attn, [w_o[0]], tm=tm_res, tn=512, mode="residual", extra=(h,), name="attn_out")
    h = ffn(h, 1)
    y_p, = rmsnorm(h, final_norm[None, :], [jnp.float32], rows_per_step=tr_p, n_row_blocks=rows_p // tr_p)
    y_s, = rmsnorm(h, final_norm[None, :], [jnp.float32], rows_per_step=ROW_PAD, row_block0=sample_blk,
                   n_row_blocks=1)

    return (y_p.reshape(n_batch, seq, d_model), y_s[:n_sample].reshape(n_sample, 1, d_model),
            conv_prompt[None], conv_sample[None],
            k_p.reshape(n_batch, seq, n_kv, HEAD_DIM), v_p.reshape(n_batch, seq, n_kv, HEAD_DIM),
            k_s.reshape(n_sample, 1, n_kv, HEAD_DIM), v_s.reshape(n_sample, 1, n_kv, HEAD_DIM))
```

```python
import functools

import jax
import jax.numpy as jnp
from jax import lax
from jax.experimental import pallas as pl
from jax.experimental.pallas import tpu as pltpu

HEAD_DIM = 128
GROUP = 4
ROT_DIM = HEAD_DIM // 4
ROT_HALF = ROT_DIM // 2
ROPE_THETA = 500000.0
MOBA_BLOCK = 256
MOBA_TOP_K = 3
CONV_WIDTH = 3
RMS_EPS = 1e-6
NEG_INF = -1e30

LANES = 128
BF16_SUBLANES = 16
VMEM_LIMIT_BYTES = 58 * 1024 * 1024

ROW_PAD = 128
BF16 = jnp.bfloat16


def _row_tile(rows, target):
    return max(t for t in range(BF16_SUBLANES, target + 1, BF16_SUBLANES) if rows % t == 0)


def _cparams(semantics):
    return pltpu.CompilerParams(dimension_semantics=semantics, vmem_limit_bytes=VMEM_LIMIT_BYTES)


def _rmsnorm_kernel(x_ref, g_ref, *o_refs):
    x = x_ref[...]
    xn = x * lax.rsqrt(jnp.mean(x * x, axis=-1, keepdims=True) + RMS_EPS)
    for i, o_ref in enumerate(o_refs):
        o_ref[...] = (xn * g_ref[i:i + 1, :]).astype(o_ref.dtype)


def rmsnorm(x, gains, out_dtypes, *, rows_per_step, row_block0=0, n_row_blocks=None):
    rows, d = x.shape
    n_out = gains.shape[0]
    if n_row_blocks is None:
        n_row_blocks = rows // rows_per_step
    out_spec = pl.BlockSpec((rows_per_step, d), lambda r: (r, 0))
    return pl.pallas_call(
        _rmsnorm_kernel,
        grid=(n_row_blocks,),
        in_specs=[pl.BlockSpec((rows_per_step, d), lambda r: (r + row_block0, 0)),
                  pl.BlockSpec((n_out, d), lambda r: (0, 0))],
        out_specs=[out_spec] * n_out,
        out_shape=[jax.ShapeDtypeStruct((n_row_blocks * rows_per_step, d), dt) for dt in out_dtypes],
        compiler_params=_cparams(("parallel",)),
        name="rmsnorm",
    )(x, gains)


def _rotary_epilogue(y, cos, sin_signed):
    lane = lax.broadcasted_iota(jnp.int32, (y.shape[0], HEAD_DIM), 1)
    first_half = lane < ROT_HALF
    outs = []
    for h in range(y.shape[1] // HEAD_DIM):
        yh = y[:, h * HEAD_DIM:(h + 1) * HEAD_DIM]
        from_above = pltpu.roll(yh, HEAD_DIM - ROT_HALF, axis=1)
        from_below = pltpu.roll(yh, ROT_HALF, axis=1)
        partner = jnp.where(first_half, from_above, from_below)
        outs.append(yh * cos + partner * sin_signed)
    return outs[0] if len(outs) == 1 else jnp.concatenate(outs, axis=1)


def _matmul_kernel(*refs, n_w, n_k, mode):
    x_ref = refs[0]
    w_refs = refs[1:1 + n_w]
    pos = 1 + n_w
    extra = ()
    if mode == "residual":
        extra = refs[pos:pos + 1]
        pos += 1
    elif mode in ("rotary", "rotary_plain"):
        extra = refs[pos:pos + 2]
        pos += 2
    n_out = 2 if mode == "rotary_plain" else 1
    o_refs = refs[pos:pos + n_out]
    acc_refs = refs[pos + n_out:]

    x = x_ref[...]
    parts = [jnp.dot(x, w_ref[...].astype(x.dtype), preferred_element_type=jnp.float32) for w_ref in w_refs]

    def finish(vals):
        if mode == "swiglu":
            g, u = vals
            ys = [(g * jax.nn.sigmoid(g)) * u]
        elif mode == "residual":
            ys = [extra[0][...] + vals[0]]
        elif mode == "rotary":
            ys = [_rotary_epilogue(vals[0], extra[0][...], extra[1][...])]
        elif mode == "rotary_plain":
            ys = [_rotary_epilogue(vals[0], extra[0][...], extra[1][...]), vals[1]]
        else:
            ys = vals
        for o_ref, y in zip(o_refs, ys):
            o_ref[...] = y.astype(o_ref.dtype)

    if n_k == 1:
        finish(parts)
        return

    k = pl.program_id(2)

    @pl.when(k == 0)
    def _():
        for acc_ref, p in zip(acc_refs, parts):
            acc_ref[...] = p

    if n_k > 2:
        @pl.when(jnp.logical_and(k > 0, k < n_k - 1))
        def _():
            for acc_ref, p in zip(acc_refs, parts):
                acc_ref[...] += p

    @pl.when(k == n_k - 1)
    def _():
        finish([acc_ref[...] + p for acc_ref, p in zip(acc_refs, parts)])


def matmul(x, ws, *, tm, tn, tk=None, layer=None, mode="plain", extra=(), out_dtype=jnp.float32,
           name="matmul", row_block0=0, n_row_blocks=None, x_buffers=2):
    m, k_dim = x.shape
    n = ws[0].shape[-1]
    tn = min(tn, n)
    tk = k_dim if tk is None else tk
    n_k = k_dim // tk
    if n_row_blocks is None:
        n_row_blocks = m // tm
    assert n % tn == 0 and k_dim % tk == 0 and (row_block0 + n_row_blocks) * tm <= m
    x_mode = {} if x_buffers == 2 else {"pipeline_mode": pl.Buffered(x_buffers)}
    in_specs = [pl.BlockSpec((tm, tk), lambda i, j, k: (i + row_block0, k), **x_mode)]
    if layer is None:
        in_specs += [pl.BlockSpec((tk, tn), lambda i, j, k: (k, j)) for _ in ws]
    else:
        in_specs += [pl.BlockSpec((None, tk, tn), lambda i, j, k: (layer, k, j)) for _ in ws]
    if mode == "residual":
        in_specs.append(pl.BlockSpec((tm, tn), lambda i, j, k: (i, j)))
    elif mode in ("rotary", "rotary_plain"):
        in_specs += [pl.BlockSpec((tm, HEAD_DIM), lambda i, j, k: (i, 0))] * 2
    n_out = 2 if mode == "rotary_plain" else 1
    out_spec = pl.BlockSpec((tm, tn), lambda i, j, k: (i, j))
    out_shape = jax.ShapeDtypeStruct((n_row_blocks * tm, n), out_dtype)
    scratch = [pltpu.VMEM((tm, tn), jnp.float32) for _ in ws] if n_k > 1 else []
    outs = pl.pallas_call(
        functools.partial(_matmul_kernel, n_w=len(ws), n_k=n_k, mode=mode),
        grid=(n_row_blocks, n // tn, n_k),
        in_specs=in_specs,
        out_specs=[out_spec] * n_out,
        out_shape=[out_shape] * n_out,
        scratch_shapes=scratch,
        compiler_params=_cparams(("parallel", "parallel", "arbitrary")),
        name=name,
    )(x, *ws, *extra)
    return outs if n_out > 1 else outs[0]


def _conv_gate_kernel(b_ref, c_ref, xin_ref, bs_ref, cs_ref, xs_ref, st0_ref, st1_ref, w_ref,
                      g_ref, state_ref, us_ref, *, n_batch):
    step = pl.program_id(1)

    @pl.when(step < n_batch)
    def _():
        u = c_ref[...] * xin_ref[...]
        t = lax.broadcasted_iota(jnp.int32, u.shape, 0)
        u_prev1 = jnp.where(t >= 1, pltpu.roll(u, 1, axis=0), 0.0)
        u_prev2 = jnp.where(t >= 2, pltpu.roll(u, 2, axis=0), 0.0)
        conv = w_ref[0:1, :] * u_prev2
        conv = conv + w_ref[1:2, :] * u_prev1
        conv = conv + w_ref[2:3, :] * u
        g_ref[...] = (b_ref[...] * conv).astype(g_ref.dtype)
        tail = pl.ds(u.shape[0] - (CONV_WIDTH - 1), CONV_WIDTH - 1)
        state_ref[...] = c_ref[tail, :] * xin_ref[tail, :]

    @pl.when(step == n_batch)
    def _():
        u = cs_ref[...] * xs_ref[...]
        conv = w_ref[0:1, :] * st0_ref[...]
        conv = conv + w_ref[1:2, :] * st1_ref[...]
        conv = conv + w_ref[2:3, :] * u
        g_ref[0:ROW_PAD, :] = (bs_ref[...] * conv).astype(g_ref.dtype)
        us_ref[...] = u


def conv_gate(bcx, conv_w, state_sample, *, n_batch, seq, n_sample, tc):
    rows, d3 = bcx.shape
    d = d3 // 3
    n_c = d // tc
    pad = jnp.zeros((ROW_PAD - n_sample, d), jnp.float32)
    st0 = jnp.concatenate([state_sample[:, 0, :], pad], axis=0)
    st1 = jnp.concatenate([state_sample[:, 1, :], pad], axis=0)
    sample_blk = (n_batch * seq) // ROW_PAD
    last = n_batch - 1

    def seq_spec(col0):
        return pl.BlockSpec((seq, tc), lambda j, s: (jnp.minimum(s, last), col0 + j))

    def sample_spec(col0):
        return pl.BlockSpec((ROW_PAD, tc), lambda j, s: (sample_blk, col0 + j))

    const_spec = pl.BlockSpec((ROW_PAD, tc), lambda j, s: (0, j))
    g, st_prompt, u_sample = pl.pallas_call(
        functools.partial(_conv_gate_kernel, n_batch=n_batch),
        grid=(n_c, n_batch + 1),
        in_specs=[seq_spec(0), seq_spec(n_c), seq_spec(2 * n_c),
                  sample_spec(0), sample_spec(n_c), sample_spec(2 * n_c),
                  const_spec, const_spec, pl.BlockSpec((CONV_WIDTH, tc), lambda j, s: (0, j))],
        out_specs=[pl.BlockSpec((seq, tc), lambda j, s: (s, j)),
                   pl.BlockSpec((None, CONV_WIDTH - 1, tc), lambda j, s: (jnp.minimum(s, last), 0, j)),
                   const_spec],
        out_shape=[jax.ShapeDtypeStruct((rows, d), BF16),
                   jax.ShapeDtypeStruct((n_batch, CONV_WIDTH - 1, d), jnp.float32),
                   jax.ShapeDtypeStruct((ROW_PAD, d), jnp.float32)],
        compiler_params=_cparams(("parallel", "arbitrary")),
        name="conv_gate",
    )(bcx, bcx, bcx, bcx, bcx, bcx, st0, st1, conv_w)
    st_sample = jnp.stack([state_sample[:, 1, :], u_sample[:n_sample]], axis=1)
    return g, st_prompt, st_sample


def _moba_kernel(q_ref, k_ref, v_ref, sample_ref, o_ref, *scratch, n_blocks, n_prompt_tiles):
    tile = pl.program_id(1)

    @pl.when(tile < n_prompt_tiles)
    def _():
        _moba_prompt_tile(q_ref, k_ref, v_ref, o_ref, *scratch, own=tile % n_blocks, n_blocks=n_blocks)

    @pl.when(tile == n_prompt_tiles)
    def _():
        o_ref[0:ROW_PAD, :] = sample_ref[...].astype(o_ref.dtype)


def _moba_prompt_tile(q_ref, k_ref, v_ref, o_ref, kbf_ref, vt_ref, kmean_ref, qs_ref, sel_ref,
                      s_ref, m_ref, l_ref, acc_ref, *, own, n_blocks):
    blk = MOBA_BLOCK
    nt_dims = (((1,), (1,)), ((), ()))

    @pl.when(own == 0)
    def _():
        kbf_ref[...] = k_ref[...].astype(BF16)
        for n in range(n_blocks):
            rows = slice(n * blk, (n + 1) * blk)
            kmean_ref[n:n + 1, :] = jnp.mean(k_ref[rows, :], axis=0, keepdims=True)
            vt_ref[n] = jnp.transpose(v_ref[rows, :]).astype(BF16)

    for g in range(GROUP):
        q = q_ref[:, g * HEAD_DIM:(g + 1) * HEAD_DIM]
        gate_t = lax.dot_general(kmean_ref[...], q, nt_dims, precision=lax.Precision.HIGHEST,
                                 preferred_element_type=jnp.float32)
        n_iota = lax.broadcasted_iota(jnp.int32, gate_t.shape, 0)
        beaten_by = jnp.zeros(gate_t.shape, jnp.int32)
        for n in range(n_blocks):
            row = gate_t[n:n + 1, :]
            beats = jnp.logical_or(row > gate_t, jnp.logical_and(row == gate_t, n < n_iota))
            beaten_by = beaten_by + jnp.where(beats, 1, 0) * (n < own).astype(jnp.int32)
        sel_t = jnp.logical_and(beaten_by < MOBA_TOP_K, n_iota < own)
        sel_ref[g] = jnp.where(sel_t, 1.0, 0.0)
        qs_ref[g] = (q * (HEAD_DIM ** -0.5)).astype(BF16)
    m_ref[...] = jnp.full(m_ref.shape, -jnp.inf, jnp.float32)
    l_ref[...] = jnp.zeros(l_ref.shape, jnp.float32)
    acc_ref[...] = jnp.zeros(acc_ref.shape, jnp.float32)

    def score_block(j, keep_of_head):
        k_blk = kbf_ref[pl.ds(pl.multiple_of(j * blk, blk), blk), :]
        for g in range(GROUP):
            s_t = lax.dot_general(k_blk, qs_ref[g], nt_dims, preferred_element_type=jnp.float32)
            s_t = jnp.where(keep_of_head(g), s_t, NEG_INF)
            s_ref[g, j] = s_t
            m_ref[g:g + 1, :] = jnp.maximum(m_ref[g:g + 1, :], jnp.max(s_t, axis=0, keepdims=True))

    def score_past(j, carry):
        score_block(j, lambda g: sel_ref[g, pl.ds(j, 1), :] > 0.5)
        return carry

    lax.fori_loop(0, own, score_past, 0)
    k_pos = lax.broadcasted_iota(jnp.int32, (blk, blk), 0)
    q_pos = lax.broadcasted_iota(jnp.int32, (blk, blk), 1)
    causal = k_pos <= q_pos
    score_block(own, lambda g: causal)

    def weigh_block(j, carry):
        for g in range(GROUP):
            p_t = jnp.exp(s_ref[g, j] - m_ref[g:g + 1, :])
            l_ref[g:g + 1, :] += jnp.sum(p_t, axis=0, keepdims=True)
            acc_ref[g] += jnp.dot(vt_ref[j], p_t.astype(BF16), preferred_element_type=jnp.float32)
        return carry

    lax.fori_loop(0, own + 1, weigh_block, 0)

    for g in range(GROUP):
        out_t = acc_ref[g] / l_ref[g:g + 1, :]
        o_ref[:, g * HEAD_DIM:(g + 1) * HEAD_DIM] = jnp.transpose(out_t).astype(o_ref.dtype)


def moba_attention(q, k, v, attn_sample, *, n_batch, seq):
    rows, d = q.shape
    n_kv = k.shape[1] // HEAD_DIM
    n_blocks = seq // MOBA_BLOCK
    n_prompt_tiles = n_batch * n_blocks
    gw = GROUP * HEAD_DIM
    q_spec = pl.BlockSpec((MOBA_BLOCK, gw), lambda h, t: (jnp.minimum(t, n_prompt_tiles - 1), h))
    kv_spec = pl.BlockSpec((seq, HEAD_DIM), lambda h, t: (jnp.minimum(t // n_blocks, n_batch - 1), h))
    return pl.pallas_call(
        functools.partial(_moba_kernel, n_blocks=n_blocks, n_prompt_tiles=n_prompt_tiles),
        grid=(n_kv, n_prompt_tiles + 1),
        in_specs=[q_spec, kv_spec, kv_spec, pl.BlockSpec((ROW_PAD, gw), lambda h, t: (0, h))],
        out_specs=pl.BlockSpec((MOBA_BLOCK, gw), lambda h, t: (t, h)),
        out_shape=jax.ShapeDtypeStruct((rows, d), BF16),
        scratch_shapes=[pltpu.VMEM((seq, HEAD_DIM), BF16),
                        pltpu.VMEM((n_blocks, HEAD_DIM, MOBA_BLOCK), BF16),
                        pltpu.VMEM((n_blocks, HEAD_DIM), jnp.float32),
                        pltpu.VMEM((GROUP, MOBA_BLOCK, HEAD_DIM), BF16),
                        pltpu.VMEM((GROUP, n_blocks, MOBA_BLOCK), jnp.float32),
                        pltpu.VMEM((GROUP, n_blocks, MOBA_BLOCK, MOBA_BLOCK), jnp.float32),
                        pltpu.VMEM((GROUP, MOBA_BLOCK), jnp.float32),
                        pltpu.VMEM((GROUP, MOBA_BLOCK), jnp.float32),
                        pltpu.VMEM((GROUP, HEAD_DIM, MOBA_BLOCK), jnp.float32)],
        compiler_params=_cparams(("parallel", "arbitrary")),
        name="moba_attention",
    )(q, k, v, attn_sample)


PAGES_PER_STEP = 16
GATHER_SLOTS = 8


def _page_mean_kernel(pt_ref, *refs, pages_per_block):
    del pt_ref
    page_refs, o_ref = refs[:-1], refs[-1]
    n_out = len(page_refs) // pages_per_block
    page_rows = page_refs[0].shape[0]
    for n in range(n_out):
        total = jnp.sum(page_refs[n * pages_per_block][...], axis=0)
        for r in range(1, pages_per_block):
            total = total + jnp.sum(page_refs[n * pages_per_block + r][...], axis=0)
        o_ref[n] = total / (pages_per_block * page_rows)


def paged_block_means(cache_k, page_table):
    _, page, n_kv, _ = cache_k.shape
    n_seq, n_pages = page_table.shape
    pages_per_block = MOBA_BLOCK // page
    blocks_per_step = PAGES_PER_STEP // pages_per_block
    n_steps = n_pages // PAGES_PER_STEP
    in_specs = [pl.BlockSpec((None, page, n_kv, HEAD_DIM),
                             lambda b, s, pt, r=r: (pt[b, s * PAGES_PER_STEP + r], 0, 0, 0))
                for r in range(PAGES_PER_STEP)]
    return pl.pallas_call(
        functools.partial(_page_mean_kernel, pages_per_block=pages_per_block),
        grid_spec=pltpu.PrefetchScalarGridSpec(
            num_scalar_prefetch=1,
            grid=(n_seq, n_steps),
            in_specs=in_specs,
            out_specs=pl.BlockSpec((None, blocks_per_step, n_kv, HEAD_DIM), lambda b, s, pt: (b, s, 0, 0))),
        out_shape=jax.ShapeDtypeStruct((n_seq, n_pages // pages_per_block, n_kv, HEAD_DIM), jnp.float32),
        compiler_params=_cparams(("parallel", "arbitrary")),
        name="page_means",
    )(page_table, *([cache_k] * PAGES_PER_STEP))


def _sample_topk_kernel(q_ref, kmean_ref, sel_ref):
    q = q_ref[...]
    n_heads = q.shape[0]
    n_kv, n_blocks, _ = kmean_ref.shape
    head_kv = lax.broadcasted_iota(jnp.int32, (n_heads, n_blocks), 0) // GROUP
    gate = jnp.zeros((n_heads, n_blocks), jnp.float32)
    for kv in range(n_kv):
        g = lax.dot_general(q, kmean_ref[kv], (((1,), (1,)), ((), ())),
                            precision=lax.Precision.HIGHEST, preferred_element_type=jnp.float32)
        gate = jnp.where(head_kv == kv, g, gate)
    blk = lax.broadcasted_iota(jnp.int32, gate.shape, 1).astype(jnp.float32)
    lane = lax.broadcasted_iota(jnp.int32, sel_ref.shape, 1)
    sel = jnp.zeros(sel_ref.shape, jnp.float32)
    for s in range(MOBA_TOP_K):
        best = jnp.max(gate, axis=-1, keepdims=True)
        idx = jnp.min(jnp.where(gate == best, blk, float(n_blocks)), axis=-1, keepdims=True)
        sel = jnp.where(lane == s, idx, sel)
        gate = jnp.where(blk == idx, -jnp.inf, gate)
    sel_ref[...] = sel.astype(jnp.int32)


def sample_topk(q_s, kmean_t):
    n_seq, n_heads, _ = q_s.shape
    _, n_kv, n_blocks, _ = kmean_t.shape
    return pl.pallas_call(
        _sample_topk_kernel,
        grid=(n_seq,),
        in_specs=[pl.BlockSpec((None, n_heads, HEAD_DIM), lambda b: (b, 0, 0)),
                  pl.BlockSpec((None, n_kv, n_blocks, HEAD_DIM), lambda b: (b, 0, 0, 0))],
        out_specs=pl.BlockSpec((None, n_heads, LANES), lambda b: (b, 0, 0)),
        out_shape=jax.ShapeDtypeStruct((n_seq, n_heads, LANES), jnp.int32),
        compiler_params=_cparams(("parallel",)),
        name="sample_topk",
    )(q_s, kmean_t)


def _sample_attend_kernel(pt_ref, sel_ref, q_ref, knew_ref, vnew_ref, ck_hbm, cv_hbm, o_ref,
                          kbuf, vbuf, sems, *, pages_per_block):
    n_seq, n_heads, _, _ = q_ref.shape
    n_pages_sel = kbuf.shape[1]
    n_items = n_seq * n_heads

    def page_copies(item, slot):
        b = item // n_heads
        kv = (item % n_heads) // GROUP
        copies = []
        for idx in range(n_pages_sel):
            blk = sel_ref[item * MOBA_TOP_K + idx // pages_per_block]
            pg = pt_ref[b, blk * pages_per_block + idx % pages_per_block]
            copies.append(pltpu.make_async_copy(ck_hbm.at[pg, :, kv, :], kbuf.at[slot, idx], sems.at[0, slot, idx]))
            copies.append(pltpu.make_async_copy(cv_hbm.at[pg, :, kv, :], vbuf.at[slot, idx], sems.at[1, slot, idx]))
        return copies

    n_slots = kbuf.shape[0]
    lookahead = n_slots - 1
    for item in range(min(lookahead, n_items)):
        for c in page_copies(item, item % n_slots):
            c.start()

    def step(item, carry):
        slot = item % n_slots

        @pl.when(item + lookahead < n_items)
        def _():
            for c in page_copies(item + lookahead, (item + lookahead) % n_slots):
                c.start()

        for c in page_copies(item, slot):
            c.wait()
        b = item // n_heads
        h = item % n_heads
        o_ref[b, h] = _sample_attend_one(q_ref[b, h], knew_ref[b, h // GROUP], vnew_ref[b, h // GROUP],
                                         [kbuf[slot, idx] for idx in range(n_pages_sel)],
                                         [vbuf[slot, idx] for idx in range(n_pages_sel)])
        return carry

    lax.fori_loop(0, n_items, step, 0)


def _sample_attend_one(q, k_new, v_new, k_pages, v_pages):
    q_scaled = q * (HEAD_DIM ** -0.5)
    qs = q_scaled.astype(BF16)
    qs8 = jnp.broadcast_to(q_scaled, (8, HEAD_DIM)).astype(BF16)
    scores = [lax.dot_general(qs8, kp.astype(BF16), (((1,), (1,)), ((), ())),
                              preferred_element_type=jnp.float32) for kp in k_pages]
    knew = k_new.astype(BF16).astype(jnp.float32)
    s_own = jnp.sum(qs.astype(jnp.float32) * knew, axis=-1, keepdims=True)
    m = s_own
    for s in scores:
        m = jnp.maximum(m, jnp.max(s[0:1, :], axis=-1, keepdims=True))
    p_own = jnp.exp(s_own - m)
    denom = p_own
    ps = []
    for s in scores:
        p = jnp.exp(s - m)
        ps.append(p)
        denom = denom + jnp.sum(p[0:1, :], axis=-1, keepdims=True)
    inv = 1.0 / denom
    vnew = v_new.astype(BF16).astype(jnp.float32)
    out = (p_own * inv).astype(BF16).astype(jnp.float32) * vnew
    for p, vp in zip(ps, v_pages):
        pv = jnp.dot((p * inv).astype(BF16), vp.astype(BF16), preferred_element_type=jnp.float32)
        out = out + pv[0:1, :]
    return out


def sample_attend(q_s4, k_new4, v_new4, cache_k, cache_v, page_table, sel_flat):
    page = cache_k.shape[1]
    pages_per_block = MOBA_BLOCK // page
    n_pages_sel = MOBA_TOP_K * pages_per_block

    def whole(a):
        return pl.BlockSpec(a.shape, lambda i, pt, sel: (0,) * a.ndim)

    return pl.pallas_call(
        functools.partial(_sample_attend_kernel, pages_per_block=pages_per_block),
        grid_spec=pltpu.PrefetchScalarGridSpec(
            num_scalar_prefetch=2,
            grid=(1,),
            in_specs=[whole(q_s4), whole(k_new4), whole(v_new4),
                      pl.BlockSpec(memory_space=pl.ANY), pl.BlockSpec(memory_space=pl.ANY)],
            out_specs=whole(q_s4),
            scratch_shapes=[pltpu.VMEM((GATHER_SLOTS, n_pages_sel, page, HEAD_DIM), jnp.float32),
                            pltpu.VMEM((GATHER_SLOTS, n_pages_sel, page, HEAD_DIM), jnp.float32),
                            pltpu.SemaphoreType.DMA((2, GATHER_SLOTS, n_pages_sel))]),
        out_shape=jax.ShapeDtypeStruct(q_s4.shape, jnp.float32),
        compiler_params=_cparams(("arbitrary",)),
        name="sample_attend",
    )(page_table, sel_flat, q_s4, k_new4, v_new4, cache_k, cache_v)


def _rope_tables(positions):
    inv_freq = ROPE_THETA ** (-jnp.arange(ROT_HALF, dtype=jnp.float32) * 2.0 / ROT_DIM)
    ang = positions.astype(jnp.float32)[:, None] * inv_freq[None, :]
    cos, sin = jnp.cos(ang), jnp.sin(ang)
    rest = HEAD_DIM - ROT_DIM
    ones = jnp.ones((positions.shape[0], rest), jnp.float32)
    zeros = jnp.zeros((positions.shape[0], rest), jnp.float32)
    return (jnp.concatenate([cos, cos, ones], axis=1), jnp.concatenate([-sin, sin, zeros], axis=1))


def kernel(x_prompt, x_sample, state_conv, cache_k, cache_v, page_table, a_norm, a_w_in, a_conv_w, a_w_out,
           kv_norm, w_k, w_v, b_norm, w_q, w_o, ffn_norm, w_gate, w_up, w_down, final_norm):
    n_batch, seq, d_model = x_prompt.shape
    n_sample = x_sample.shape[0]
    assert x_sample.shape[1] == 1 and n_sample <= ROW_PAD
    assert a_w_in.shape[0] == 1 and w_q.shape[0] == 1, "one short-conv layer then one MoBA layer"
    _, page, n_kv, _ = cache_k.shape
    n_heads = d_model // HEAD_DIM
    past_len = page_table.shape[1] * page
    assert past_len % MOBA_BLOCK == 0 and past_len // MOBA_BLOCK >= MOBA_TOP_K
    rows_p = n_batch * seq
    rows = rows_p + ROW_PAD
    sample_blk = rows_p // ROW_PAD

    tr, tm, tm_big = (_row_tile(rows, t) for t in (320, 1040, 1664))
    tr_p, tm_p = _row_tile(rows_p, 256), _row_tile(rows_p, 1024)

    x = jnp.concatenate([x_prompt.reshape(rows_p, d_model), x_sample.reshape(n_sample, d_model),
                         jnp.zeros((ROW_PAD - n_sample, d_model), x_prompt.dtype)], axis=0)
    pos_p = jnp.tile(jnp.arange(seq, dtype=jnp.int32), n_batch)
    pos_s = jnp.concatenate([jnp.full((n_sample,), past_len, jnp.int32),
                             jnp.zeros((ROW_PAD - n_sample,), jnp.int32)])
    rope_p, rope_s = _rope_tables(pos_p), _rope_tables(pos_s)
    rope_all = tuple(jnp.concatenate([p, s], axis=0) for p, s in zip(rope_p, rope_s))

    def ffn(h, layer):
        hn, = rmsnorm(h, ffn_norm[layer:layer + 1], [BF16], rows_per_step=tr)
        act = matmul(hn, [w_gate, w_up], layer=layer, tm=tm_big, tn=256, mode="swiglu", out_dtype=BF16,
                     name="ffn_gate_up")
        d_ff = act.shape[1]
        return matmul(act, [w_down[layer].astype(BF16)], tm=tm, tn=512, tk=d_ff // 2, mode="residual",
                      extra=(h,), name="ffn_down")

    xn, = rmsnorm(x, a_norm, [BF16], rows_per_step=tr)
    bcx = matmul(xn, [a_w_in], layer=0, tm=tm_big, tn=512, x_buffers=1, name="conv_in")
    g, conv_prompt, conv_sample = conv_gate(bcx, a_conv_w[0], state_conv[0], n_batch=n_batch, seq=seq,
                                            n_sample=n_sample, tc=256)
    h = matmul(g, [a_w_out], layer=0, tm=tm_big, tn=512, x_buffers=1, mode="residual", extra=(x,),
               name="conv_out")
    h = ffn(h, 0)

    kv_in, q_in = rmsnorm(h, jnp.stack([kv_norm, b_norm[0]]), [BF16, BF16], rows_per_step=tr)
    k_p, v_p = matmul(kv_in, [w_k, w_v], tm=tm_p, tn=256, mode="rotary_plain", extra=rope_p, name="kv_proj",
                      n_row_blocks=rows_p // tm_p)
    k_s, v_s = matmul(kv_in, [w_k, w_v], tm=ROW_PAD, tn=256, mode="rotary_plain", extra=rope_s,
                      name="kv_proj_sample", row_block0=sample_blk, n_row_blocks=1)
    q = matmul(q_in, [w_q], layer=0, tm=tm_big, tn=512, x_buffers=1, mode="rotary", extra=rope_all,
               name="q_proj")

    kmean = paged_block_means(cache_k, page_table)
    q_s = q[rows_p:rows_p + n_sample].reshape(n_sample, n_heads, HEAD_DIM)
    sel = sample_topk(q_s, kmean.transpose(0, 2, 1, 3))[:, :, :MOBA_TOP_K].reshape(-1)
    k_s = k_s[:n_sample].reshape(n_sample, n_kv, 1, HEAD_DIM)
    v_s = v_s[:n_sample].reshape(n_sample, n_kv, 1, HEAD_DIM)
    attn_s = sample_attend(q_s.reshape(n_sample, n_heads, 1, HEAD_DIM), k_s, v_s, cache_k, cache_v,
                           page_table, sel)
    attn_s = jnp.concatenate([attn_s.reshape(n_sample, d_model),
                              jnp.zeros((ROW_PAD - n_sample, d_model), jnp.float32)], axis=0)
    attn = moba_attention(q, k_p, v_p, attn_s, n_batch=n_batch, seq=seq)

    h = matmul(attn, [w_o], layer=0, tm=tm_big, tn=512, x_buffers=1, mode="residual", extra=(h,),
               name="attn_out")
    h = ffn(h, 1)
    y_p, = rmsnorm(h, final_norm[None, :], [jnp.float32], rows_per_step=tr_p, n_row_blocks=rows_p // tr_p)
    y_s, = rmsnorm(h, final_norm[None, :], [jnp.float32], rows_per_step=ROW_PAD, row_block0=sample_blk,
                   n_row_blocks=1)

    return (y_p.reshape(n_batch, seq, d_model), y_s[:n_sample].reshape(n_sample, 1, d_model),
            conv_prompt[None], conv_sample[None],
            k_p.reshape(n_batch, seq, n_kv, HEAD_DIM), v_p.reshape(n_batch, seq, n_kv, HEAD_DIM),
            k_s.reshape(n_sample, 1, n_kv, HEAD_DIM), v_s.reshape(n_sample, 1, n_kv, HEAD_DIM))
```

```python
import functools

import jax
import jax.numpy as jnp
from jax import lax
from jax.experimental import pallas as pl
from jax.experimental.pallas import tpu as pltpu

HEAD_DIM = 128
GROUP = 4
ROT_DIM = HEAD_DIM // 4
ROT_HALF = ROT_DIM // 2
ROPE_THETA = 500000.0
MOBA_BLOCK = 256
MOBA_TOP_K = 3
CONV_WIDTH = 3
RMS_EPS = 1e-6
NEG_INF = -1e30

LANES = 128
BF16_SUBLANES = 16
VMEM_LIMIT_BYTES = 58 * 1024 * 1024

ROW_PAD = 128
BF16 = jnp.bfloat16


def _row_tile(rows, target):
    return max(t for t in range(BF16_SUBLANES, target + 1, BF16_SUBLANES) if rows % t == 0)


def _cparams(semantics):
    return pltpu.CompilerParams(dimension_semantics=semantics, vmem_limit_bytes=VMEM_LIMIT_BYTES)


def _rmsnorm_kernel(x_ref, g_ref, *o_refs):
    x = x_ref[...]
    xn = x * lax.rsqrt(jnp.mean(x * x, axis=-1, keepdims=True) + RMS_EPS)
    for i, o_ref in enumerate(o_refs):
        o_ref[...] = (xn * g_ref[i:i + 1, :]).astype(o_ref.dtype)


def rmsnorm(x, gains, out_dtypes, *, rows_per_step, row_block0=0, n_row_blocks=None):
    rows, d = x.shape
    n_out = gains.shape[0]
    if n_row_blocks is None:
        n_row_blocks = rows // rows_per_step
    out_spec = pl.BlockSpec((rows_per_step, d), lambda r: (r, 0))
    return pl.pallas_call(
        _rmsnorm_kernel,
        grid=(n_row_blocks,),
        in_specs=[pl.BlockSpec((rows_per_step, d), lambda r: (r + row_block0, 0)),
                  pl.BlockSpec((n_out, d), lambda r: (0, 0))],
        out_specs=[out_spec] * n_out,
        out_shape=[jax.ShapeDtypeStruct((n_row_blocks * rows_per_step, d), dt) for dt in out_dtypes],
        compiler_params=_cparams(("parallel",)),
        name="rmsnorm",
    )(x, gains)


def _rotary_epilogue(y, cos, sin_signed):
    lane = lax.broadcasted_iota(jnp.int32, (y.shape[0], HEAD_DIM), 1)
    first_half = lane < ROT_HALF
    outs = []
    for h in range(y.shape[1] // HEAD_DIM):
        yh = y[:, h * HEAD_DIM:(h + 1) * HEAD_DIM]
        from_above = pltpu.roll(yh, HEAD_DIM - ROT_HALF, axis=1)
        from_below = pltpu.roll(yh, ROT_HALF, axis=1)
        partner = jnp.where(first_half, from_above, from_below)
        outs.append(yh * cos + partner * sin_signed)
    return outs[0] if len(outs) == 1 else jnp.concatenate(outs, axis=1)


def _matmul_kernel(*refs, n_w, mode, normed_x, n_gains, side_cast):
    refs = list(refs)
    x_ref = refs.pop(0)
    w_refs = [refs.pop(0) for _ in range(n_w)]
    ssq_in_ref = refs.pop(0) if normed_x else None
    res_ref = refs.pop(0) if mode == "residual" else None
    gains_ref = refs.pop(0) if n_gains else None
    rope_refs = [refs.pop(0) for _ in range(2)] if mode in ("rotary", "rotary_plain") else None
    side_in_ref = refs.pop(0) if side_cast else None
    o_refs = [refs.pop(0) for _ in range(2 if mode == "rotary_plain" else 1)]
    scaled_refs = [refs.pop(0) for _ in range(n_gains)]
    ssq_out_ref = refs.pop(0) if n_gains else None
    side_out_ref = refs.pop(0) if side_cast else None
    assert not refs
    i, j = pl.program_id(0), pl.program_id(1)

    x = x_ref[...]
    vals = [jnp.dot(x, w_ref[...].astype(x.dtype), preferred_element_type=jnp.float32) for w_ref in w_refs]
    if normed_x:
        inv_rms = lax.rsqrt(ssq_in_ref[:, 0:1] * (1.0 / x.shape[1]) + RMS_EPS)
        vals = [v * inv_rms for v in vals]

    if mode == "swiglu":
        g, u = vals
        ys = [(g * jax.nn.sigmoid(g)) * u]
    elif mode == "residual":
        ys = [res_ref[...] + vals[0]]
    elif mode == "rotary":
        ys = [_rotary_epilogue(vals[0], rope_refs[0][...], rope_refs[1][...])]
    elif mode == "rotary_plain":
        ys = [_rotary_epilogue(vals[0], rope_refs[0][...], rope_refs[1][...]), vals[1]]
    else:
        ys = vals
    for o_ref, y in zip(o_refs, ys):
        o_ref[...] = y.astype(o_ref.dtype)

    if n_gains:
        h = ys[0]
        for n, scaled_ref in enumerate(scaled_refs):
            scaled_ref[...] = (h * gains_ref[n:n + 1, :]).astype(scaled_ref.dtype)
        part = jnp.broadcast_to(jnp.sum(h * h, axis=1, keepdims=True), ssq_out_ref.shape)

        @pl.when(j == 0)
        def _():
            ssq_out_ref[...] = part

        @pl.when(j > 0)
        def _():
            ssq_out_ref[...] += part

    if side_cast:
        @pl.when(i == 0)
        def _():
            side_out_ref[...] = side_in_ref[...].astype(side_out_ref.dtype)


def matmul(x, ws, *, tm, tn, layer=None, mode="plain", extra=(), out_dtype=jnp.float32, name="matmul",
           row_block0=0, n_row_blocks=None, x_buffers=2, x_ssq=None, next_gains=None, side_cast=None):
    m, k_dim = x.shape
    n = ws[0].shape[-1]
    tn = min(tn, n)
    n_j = n // tn
    if n_row_blocks is None:
        n_row_blocks = m // tm
    rows_out = n_row_blocks * tm
    assert n % tn == 0 and (row_block0 + n_row_blocks) * tm <= m
    assert next_gains is None or mode == "residual"
    n_gains = 0 if next_gains is None else next_gains.shape[0]

    out_tile = pl.BlockSpec((tm, tn), lambda i, j: (i, j))

    def per_row(width):
        return pl.BlockSpec((tm, width), lambda i, j: (i, 0))

    x_mode = {} if x_buffers == 2 else {"pipeline_mode": pl.Buffered(x_buffers)}
    operands = [x]
    in_specs = [pl.BlockSpec((tm, k_dim), lambda i, j: (i + row_block0, 0), **x_mode)]
    operands += ws
    if layer is None:
        in_specs += [pl.BlockSpec((k_dim, tn), lambda i, j: (0, j)) for _ in ws]
    else:
        in_specs += [pl.BlockSpec((None, k_dim, tn), lambda i, j: (layer, 0, j)) for _ in ws]
    if x_ssq is not None:
        operands.append(x_ssq)
        in_specs.append(pl.BlockSpec((tm, LANES), lambda i, j: (i + row_block0, 0)))
    if mode == "residual":
        operands.append(extra[0])
        in_specs.append(out_tile)
    if n_gains:
        operands.append(next_gains)
        in_specs.append(pl.BlockSpec((n_gains, tn), lambda i, j: (0, j)))
    if mode in ("rotary", "rotary_plain"):
        operands += extra
        in_specs += [per_row(HEAD_DIM)] * 2

    n_main = 2 if mode == "rotary_plain" else 1
    out_specs = [out_tile] * n_main
    out_shapes = [jax.ShapeDtypeStruct((rows_out, n), out_dtype)] * n_main
    if n_gains:
        out_specs += [out_tile] * n_gains + [per_row(LANES)]
        out_shapes += [jax.ShapeDtypeStruct((rows_out, n), BF16)] * n_gains
        out_shapes.append(jax.ShapeDtypeStruct((rows_out, LANES), jnp.float32))
    if side_cast is not None:
        side_w, side_layer = side_cast
        _, side_rows, side_cols = side_w.shape
        slab = side_rows // n_j
        assert side_rows % n_j == 0 and slab % BF16_SUBLANES == 0

        def slab_index(i, j):
            return jnp.where(i == 0, j, n_j - 1)

        operands.append(side_w)
        in_specs.append(pl.BlockSpec((None, slab, side_cols), lambda i, j: (side_layer, slab_index(i, j), 0)))
        out_specs.append(pl.BlockSpec((slab, side_cols), lambda i, j: (slab_index(i, j), 0)))
        out_shapes.append(jax.ShapeDtypeStruct((side_rows, side_cols), BF16))

    outs = pl.pallas_call(
        functools.partial(_matmul_kernel, n_w=len(ws), mode=mode, normed_x=x_ssq is not None,
                          n_gains=n_gains, side_cast=side_cast is not None),
        grid=(n_row_blocks, n_j),
        in_specs=in_specs,
        out_specs=out_specs,
        out_shape=out_shapes,
        compiler_params=_cparams(("arbitrary" if side_cast is not None else "parallel",
                                  "arbitrary" if (n_gains or side_cast is not None) else "parallel")),
        name=name,
    )(*operands)
    return outs if len(outs) > 1 else outs[0]


def _conv_gate_kernel(b_ref, c_ref, xin_ref, bs_ref, cs_ref, xs_ref, st0_ref, st1_ref, w_ref,
                      g_ref, state_ref, us_ref, *, n_batch):
    step = pl.program_id(1)

    @pl.when(step < n_batch)
    def _():
        u = c_ref[...] * xin_ref[...]
        t = lax.broadcasted_iota(jnp.int32, u.shape, 0)
        u_prev1 = jnp.where(t >= 1, pltpu.roll(u, 1, axis=0), 0.0)
        u_prev2 = jnp.where(t >= 2, pltpu.roll(u, 2, axis=0), 0.0)
        conv = w_ref[0:1, :] * u_prev2
        conv = conv + w_ref[1:2, :] * u_prev1
        conv = conv + w_ref[2:3, :] * u
        g_ref[...] = (b_ref[...] * conv).astype(g_ref.dtype)
        tail = pl.ds(u.shape[0] - (CONV_WIDTH - 1), CONV_WIDTH - 1)
        state_ref[...] = c_ref[tail, :] * xin_ref[tail, :]

    @pl.when(step == n_batch)
    def _():
        u = cs_ref[...] * xs_ref[...]
        conv = w_ref[0:1, :] * st0_ref[...]
        conv = conv + w_ref[1:2, :] * st1_ref[...]
        conv = conv + w_ref[2:3, :] * u
        g_ref[0:ROW_PAD, :] = (bs_ref[...] * conv).astype(g_ref.dtype)
        us_ref[...] = u


def conv_gate(bcx, conv_w, state_sample, *, n_batch, seq, n_sample, tc):
    rows, d3 = bcx.shape
    d = d3 // 3
    n_c = d // tc
    pad = jnp.zeros((ROW_PAD - n_sample, d), jnp.float32)
    st0 = jnp.concatenate([state_sample[:, 0, :], pad], axis=0)
    st1 = jnp.concatenate([state_sample[:, 1, :], pad], axis=0)
    sample_blk = (n_batch * seq) // ROW_PAD
    last = n_batch - 1

    def seq_spec(col0):
        return pl.BlockSpec((seq, tc), lambda j, s: (jnp.minimum(s, last), col0 + j))

    def sample_spec(col0):
        return pl.BlockSpec((ROW_PAD, tc), lambda j, s: (sample_blk, col0 + j))

    const_spec = pl.BlockSpec((ROW_PAD, tc), lambda j, s: (0, j))
    g, st_prompt, u_sample = pl.pallas_call(
        functools.partial(_conv_gate_kernel, n_batch=n_batch),
        grid=(n_c, n_batch + 1),
        in_specs=[seq_spec(0), seq_spec(n_c), seq_spec(2 * n_c),
                  sample_spec(0), sample_spec(n_c), sample_spec(2 * n_c),
                  const_spec, const_spec, pl.BlockSpec((CONV_WIDTH, tc), lambda j, s: (0, j))],
        out_specs=[pl.BlockSpec((seq, tc), lambda j, s: (s, j)),
                   pl.BlockSpec((None, CONV_WIDTH - 1, tc), lambda j, s: (jnp.minimum(s, last), 0, j)),
                   const_spec],
        out_shape=[jax.ShapeDtypeStruct((rows, d), BF16),
                   jax.ShapeDtypeStruct((n_batch, CONV_WIDTH - 1, d), jnp.float32),
                   jax.ShapeDtypeStruct((ROW_PAD, d), jnp.float32)],
        compiler_params=_cparams(("parallel", "arbitrary")),
        name="conv_gate",
    )(bcx, bcx, bcx, bcx, bcx, bcx, st0, st1, conv_w)
    st_sample = jnp.stack([state_sample[:, 1, :], u_sample[:n_sample]], axis=1)
    return g, st_prompt, st_sample


def _moba_kernel(q_ref, k_ref, v_ref, sample_ref, o_ref, *scratch, n_blocks, n_prompt_tiles):
    tile = pl.program_id(1)

    @pl.when(tile < n_prompt_tiles)
    def _():
        _moba_prompt_tile(q_ref, k_ref, v_ref, o_ref, *scratch, own=tile % n_blocks, n_blocks=n_blocks)

    @pl.when(tile == n_prompt_tiles)
    def _():
        o_ref[0:ROW_PAD, :] = sample_ref[...].astype(o_ref.dtype)


def _moba_prompt_tile(q_ref, k_ref, v_ref, o_ref, kbf_ref, vt_ref, kmean_ref, qs_ref, sel_ref,
                      s_ref, m_ref, l_ref, acc_ref, *, own, n_blocks):
    blk = MOBA_BLOCK
    nt_dims = (((1,), (1,)), ((), ()))

    @pl.when(own == 0)
    def _():
        kbf_ref[...] = k_ref[...].astype(BF16)
        for n in range(n_blocks):
            rows = slice(n * blk, (n + 1) * blk)
            kmean_ref[n:n + 1, :] = jnp.mean(k_ref[rows, :], axis=0, keepdims=True)
            vt_ref[n] = jnp.transpose(v_ref[rows, :]).astype(BF16)

    for g in range(GROUP):
        q = q_ref[:, g * HEAD_DIM:(g + 1) * HEAD_DIM]
        gate_t = lax.dot_general(kmean_ref[...], q, nt_dims, precision=lax.Precision.HIGHEST,
                                 preferred_element_type=jnp.float32)
        n_iota = lax.broadcasted_iota(jnp.int32, gate_t.shape, 0)
        beaten_by = jnp.zeros(gate_t.shape, jnp.int32)
        for n in range(n_blocks):
            row = gate_t[n:n + 1, :]
            beats = jnp.logical_or(row > gate_t, jnp.logical_and(row == gate_t, n < n_iota))
            beaten_by = beaten_by + jnp.where(beats, 1, 0) * (n < own).astype(jnp.int32)
        sel_t = jnp.logical_and(beaten_by < MOBA_TOP_K, n_iota < own)
        sel_ref[g] = jnp.where(sel_t, 1.0, 0.0)
        qs_ref[g] = (q * (HEAD_DIM ** -0.5)).astype(BF16)
    m_ref[...] = jnp.full(m_ref.shape, -jnp.inf, jnp.float32)
    l_ref[...] = jnp.zeros(l_ref.shape, jnp.float32)
    acc_ref[...] = jnp.zeros(acc_ref.shape, jnp.float32)

    def score_block(j, keep_of_head):
        k_blk = kbf_ref[pl.ds(pl.multiple_of(j * blk, blk), blk), :]
        for g in range(GROUP):
            s_t = lax.dot_general(k_blk, qs_ref[g], nt_dims, preferred_element_type=jnp.float32)
            s_t = jnp.where(keep_of_head(g), s_t, NEG_INF)
            s_ref[g, j] = s_t
            m_ref[g:g + 1, :] = jnp.maximum(m_ref[g:g + 1, :], jnp.max(s_t, axis=0, keepdims=True))

    def score_past(j, carry):
        score_block(j, lambda g: sel_ref[g, pl.ds(j, 1), :] > 0.5)
        return carry

    lax.fori_loop(0, own, score_past, 0)
    k_pos = lax.broadcasted_iota(jnp.int32, (blk, blk), 0)
    q_pos = lax.broadcasted_iota(jnp.int32, (blk, blk), 1)
    causal = k_pos <= q_pos
    score_block(own, lambda g: causal)

    def weigh_block(j, carry):
        for g in range(GROUP):
            p_t = jnp.exp(s_ref[g, j] - m_ref[g:g + 1, :])
            l_ref[g:g + 1, :] += jnp.sum(p_t, axis=0, keepdims=True)
            acc_ref[g] += jnp.dot(vt_ref[j], p_t.astype(BF16), preferred_element_type=jnp.float32)
        return carry

    lax.fori_loop(0, own + 1, weigh_block, 0)

    for g in range(GROUP):
        out_t = acc_ref[g] / l_ref[g:g + 1, :]
        o_ref[:, g * HEAD_DIM:(g + 1) * HEAD_DIM] = jnp.transpose(out_t).astype(o_ref.dtype)


def moba_attention(q, k, v, attn_sample, *, n_batch, seq):
    rows, d = q.shape
    n_kv = k.shape[1] // HEAD_DIM
    n_blocks = seq // MOBA_BLOCK
    n_prompt_tiles = n_batch * n_blocks
    gw = GROUP * HEAD_DIM
    q_spec = pl.BlockSpec((MOBA_BLOCK, gw), lambda h, t: (jnp.minimum(t, n_prompt_tiles - 1), h))
    kv_spec = pl.BlockSpec((seq, HEAD_DIM), lambda h, t: (jnp.minimum(t // n_blocks, n_batch - 1), h))
    return pl.pallas_call(
        functools.partial(_moba_kernel, n_blocks=n_blocks, n_prompt_tiles=n_prompt_tiles),
        grid=(n_kv, n_prompt_tiles + 1),
        in_specs=[q_spec, kv_spec, kv_spec, pl.BlockSpec((ROW_PAD, gw), lambda h, t: (0, h))],
        out_specs=pl.BlockSpec((MOBA_BLOCK, gw), lambda h, t: (t, h)),
        out_shape=jax.ShapeDtypeStruct((rows, d), BF16),
        scratch_shapes=[pltpu.VMEM((seq, HEAD_DIM), BF16),
                        pltpu.VMEM((n_blocks, HEAD_DIM, MOBA_BLOCK), BF16),
                        pltpu.VMEM((n_blocks, HEAD_DIM), jnp.float32),
                        pltpu.VMEM((GROUP, MOBA_BLOCK, HEAD_DIM), BF16),
                        pltpu.VMEM((GROUP, n_blocks, MOBA_BLOCK), jnp.float32),
                        pltpu.VMEM((GROUP, n_blocks, MOBA_BLOCK, MOBA_BLOCK), jnp.float32),
                        pltpu.VMEM((GROUP, MOBA_BLOCK), jnp.float32),
                        pltpu.VMEM((GROUP, MOBA_BLOCK), jnp.float32),
                        pltpu.VMEM((GROUP, HEAD_DIM, MOBA_BLOCK), jnp.float32)],
        compiler_params=_cparams(("parallel", "arbitrary")),
        name="moba_attention",
    )(q, k, v, attn_sample)


PAGES_PER_STEP = 16
GATHER_SLOTS = 8


def _page_mean_kernel(pt_ref, *refs, pages_per_block):
    del pt_ref
    page_refs, o_ref = refs[:-1], refs[-1]
    n_out = len(page_refs) // pages_per_block
    page_rows = page_refs[0].shape[0]
    for n in range(n_out):
        total = jnp.sum(page_refs[n * pages_per_block][...], axis=0)
        for r in range(1, pages_per_block):
            total = total + jnp.sum(page_refs[n * pages_per_block + r][...], axis=0)
        o_ref[n] = total / (pages_per_block * page_rows)


def paged_block_means(cache_k, page_table):
    _, page, n_kv, _ = cache_k.shape
    n_seq, n_pages = page_table.shape
    pages_per_block = MOBA_BLOCK // page
    blocks_per_step = PAGES_PER_STEP // pages_per_block
    n_steps = n_pages // PAGES_PER_STEP
    in_specs = [pl.BlockSpec((None, page, n_kv, HEAD_DIM),
                             lambda b, s, pt, r=r: (pt[b, s * PAGES_PER_STEP + r], 0, 0, 0))
                for r in range(PAGES_PER_STEP)]
    return pl.pallas_call(
        functools.partial(_page_mean_kernel, pages_per_block=pages_per_block),
        grid_spec=pltpu.PrefetchScalarGridSpec(
            num_scalar_prefetch=1,
            grid=(n_seq, n_steps),
            in_specs=in_specs,
            out_specs=pl.BlockSpec((None, blocks_per_step, n_kv, HEAD_DIM), lambda b, s, pt: (b, s, 0, 0))),
        out_shape=jax.ShapeDtypeStruct((n_seq, n_pages // pages_per_block, n_kv, HEAD_DIM), jnp.float32),
        compiler_params=_cparams(("parallel", "arbitrary")),
        name="page_means",
    )(page_table, *([cache_k] * PAGES_PER_STEP))


def _sample_topk_kernel(q_ref, kmean_ref, sel_ref):
    q = q_ref[...]
    n_heads = q.shape[0]
    n_kv, n_blocks, _ = kmean_ref.shape
    head_kv = lax.broadcasted_iota(jnp.int32, (n_heads, n_blocks), 0) // GROUP
    gate = jnp.zeros((n_heads, n_blocks), jnp.float32)
    for kv in range(n_kv):
        g = lax.dot_general(q, kmean_ref[kv], (((1,), (1,)), ((), ())),
                            precision=lax.Precision.HIGHEST, preferred_element_type=jnp.float32)
        gate = jnp.where(head_kv == kv, g, gate)
    blk = lax.broadcasted_iota(jnp.int32, gate.shape, 1).astype(jnp.float32)
    lane = lax.broadcasted_iota(jnp.int32, sel_ref.shape, 1)
    sel = jnp.zeros(sel_ref.shape, jnp.float32)
    for s in range(MOBA_TOP_K):
        best = jnp.max(gate, axis=-1, keepdims=True)
        idx = jnp.min(jnp.where(gate == best, blk, float(n_blocks)), axis=-1, keepdims=True)
        sel = jnp.where(lane == s, idx, sel)
        gate = jnp.where(blk == idx, -jnp.inf, gate)
    sel_ref[...] = sel.astype(jnp.int32)


def sample_topk(q_s, kmean_t):
    n_seq, n_heads, _ = q_s.shape
    _, n_kv, n_blocks, _ = kmean_t.shape
    return pl.pallas_call(
        _sample_topk_kernel,
        grid=(n_seq,),
        in_specs=[pl.BlockSpec((None, n_heads, HEAD_DIM), lambda b: (b, 0, 0)),
                  pl.BlockSpec((None, n_kv, n_blocks, HEAD_DIM), lambda b: (b, 0, 0, 0))],
        out_specs=pl.BlockSpec((None, n_heads, LANES), lambda b: (b, 0, 0)),
        out_shape=jax.ShapeDtypeStruct((n_seq, n_heads, LANES), jnp.int32),
        compiler_params=_cparams(("parallel",)),
        name="sample_topk",
    )(q_s, kmean_t)


def _sample_attend_kernel(pt_ref, sel_ref, q_ref, knew_ref, vnew_ref, ck_hbm, cv_hbm, o_ref,
                          kbuf, vbuf, sems, *, pages_per_block):
    n_seq, n_heads, _, _ = q_ref.shape
    n_pages_sel = kbuf.shape[1]
    n_items = n_seq * n_heads

    def page_copies(item, slot):
        b = item // n_heads
        kv = (item % n_heads) // GROUP
        copies = []
        for idx in range(n_pages_sel):
            blk = sel_ref[item * MOBA_TOP_K + idx // pages_per_block]
            pg = pt_ref[b, blk * pages_per_block + idx % pages_per_block]
            copies.append(pltpu.make_async_copy(ck_hbm.at[pg, :, kv, :], kbuf.at[slot, idx], sems.at[0, slot, idx]))
            copies.append(pltpu.make_async_copy(cv_hbm.at[pg, :, kv, :], vbuf.at[slot, idx], sems.at[1, slot, idx]))
        return copies

    n_slots = kbuf.shape[0]
    lookahead = n_slots - 1
    for item in range(min(lookahead, n_items)):
        for c in page_copies(item, item % n_slots):
            c.start()

    def step(item, carry):
        slot = item % n_slots

        @pl.when(item + lookahead < n_items)
        def _():
            for c in page_copies(item + lookahead, (item + lookahead) % n_slots):
                c.start()

        for c in page_copies(item, slot):
            c.wait()
        b = item // n_heads
        h = item % n_heads
        o_ref[b, h] = _sample_attend_one(q_ref[b, h], knew_ref[b, h // GROUP], vnew_ref[b, h // GROUP],
                                         [kbuf[slot, idx] for idx in range(n_pages_sel)],
                                         [vbuf[slot, idx] for idx in range(n_pages_sel)])
        return carry

    lax.fori_loop(0, n_items, step, 0)


def _sample_attend_one(q, k_new, v_new, k_pages, v_pages):
    q_scaled = q * (HEAD_DIM ** -0.5)
    qs = q_scaled.astype(BF16)
    qs8 = jnp.broadcast_to(q_scaled, (8, HEAD_DIM)).astype(BF16)
    scores = [lax.dot_general(qs8, kp.astype(BF16), (((1,), (1,)), ((), ())),
                              preferred_element_type=jnp.float32) for kp in k_pages]
    knew = k_new.astype(BF16).astype(jnp.float32)
    s_own = jnp.sum(qs.astype(jnp.float32) * knew, axis=-1, keepdims=True)
    m = s_own
    for s in scores:
        m = jnp.maximum(m, jnp.max(s[0:1, :], axis=-1, keepdims=True))
    p_own = jnp.exp(s_own - m)
    denom = p_own
    ps = []
    for s in scores:
        p = jnp.exp(s - m)
        ps.append(p)
        denom = denom + jnp.sum(p[0:1, :], axis=-1, keepdims=True)
    inv = 1.0 / denom
    vnew = v_new.astype(BF16).astype(jnp.float32)
    out = (p_own * inv).astype(BF16).astype(jnp.float32) * vnew
    for p, vp in zip(ps, v_pages):
        pv = jnp.dot((p * inv).astype(BF16), vp.astype(BF16), preferred_element_type=jnp.float32)
        out = out + pv[0:1, :]
    return out


def sample_attend(q_s4, k_new4, v_new4, cache_k, cache_v, page_table, sel_flat):
    page = cache_k.shape[1]
    pages_per_block = MOBA_BLOCK // page
    n_pages_sel = MOBA_TOP_K * pages_per_block

    def whole(a):
        return pl.BlockSpec(a.shape, lambda i, pt, sel: (0,) * a.ndim)

    return pl.pallas_call(
        functools.partial(_sample_attend_kernel, pages_per_block=pages_per_block),
        grid_spec=pltpu.PrefetchScalarGridSpec(
            num_scalar_prefetch=2,
            grid=(1,),
            in_specs=[whole(q_s4), whole(k_new4), whole(v_new4),
                      pl.BlockSpec(memory_space=pl.ANY), pl.BlockSpec(memory_space=pl.ANY)],
            out_specs=whole(q_s4),
            scratch_shapes=[pltpu.VMEM((GATHER_SLOTS, n_pages_sel, page, HEAD_DIM), jnp.float32),
                            pltpu.VMEM((GATHER_SLOTS, n_pages_sel, page, HEAD_DIM), jnp.float32),
                            pltpu.SemaphoreType.DMA((2, GATHER_SLOTS, n_pages_sel))]),
        out_shape=jax.ShapeDtypeStruct(q_s4.shape, jnp.float32),
        compiler_params=_cparams(("arbitrary",)),
        name="sample_attend",
    )(page_table, sel_flat, q_s4, k_new4, v_new4, cache_k, cache_v)


def _rope_tables(positions):
    inv_freq = ROPE_THETA ** (-jnp.arange(ROT_HALF, dtype=jnp.float32) * 2.0 / ROT_DIM)
    ang = positions.astype(jnp.float32)[:, None] * inv_freq[None, :]
    cos, sin = jnp.cos(ang), jnp.sin(ang)
    rest = HEAD_DIM - ROT_DIM
    ones = jnp.ones((positions.shape[0], rest), jnp.float32)
    zeros = jnp.zeros((positions.shape[0], rest), jnp.float32)
    return (jnp.concatenate([cos, cos, ones], axis=1), jnp.concatenate([-sin, sin, zeros], axis=1))


def kernel(x_prompt, x_sample, state_conv, cache_k, cache_v, page_table, a_norm, a_w_in, a_conv_w, a_w_out,
           kv_norm, w_k, w_v, b_norm, w_q, w_o, ffn_norm, w_gate, w_up, w_down, final_norm):
    n_batch, seq, d_model = x_prompt.shape
    n_sample = x_sample.shape[0]
    assert x_sample.shape[1] == 1 and n_sample <= ROW_PAD
    assert a_w_in.shape[0] == 1 and w_q.shape[0] == 1, "one short-conv layer then one MoBA layer"
    _, page, n_kv, _ = cache_k.shape
    n_heads = d_model // HEAD_DIM
    past_len = page_table.shape[1] * page
    assert past_len % MOBA_BLOCK == 0 and past_len // MOBA_BLOCK >= MOBA_TOP_K
    rows_p = n_batch * seq
    rows = rows_p + ROW_PAD
    sample_blk = rows_p // ROW_PAD

    tr, tm_down, tm_big = (_row_tile(rows, t) for t in (320, 832, 1664))
    tr_p, tm_p = _row_tile(rows_p, 256), _row_tile(rows_p, 1024)

    x = jnp.concatenate([x_prompt.reshape(rows_p, d_model), x_sample.reshape(n_sample, d_model),
                         jnp.zeros((ROW_PAD - n_sample, d_model), x_prompt.dtype)], axis=0)
    pos_p = jnp.tile(jnp.arange(seq, dtype=jnp.int32), n_batch)
    pos_s = jnp.concatenate([jnp.full((n_sample,), past_len, jnp.int32),
                             jnp.zeros((ROW_PAD - n_sample,), jnp.int32)])
    rope_p, rope_s = _rope_tables(pos_p), _rope_tables(pos_s)
    rope_all = tuple(jnp.concatenate([p, s], axis=0) for p, s in zip(rope_p, rope_s))

    def ffn(h, h_scaled, h_ssq, layer, next_gains):
        act, w_down_bf16 = matmul(h_scaled, [w_gate, w_up], layer=layer, tm=tm_big, tn=256, x_buffers=1,
                                  mode="swiglu", out_dtype=BF16, x_ssq=h_ssq, side_cast=(w_down, layer),
                                  name="ffn_gate_up")
        return matmul(act, [w_down_bf16], tm=tm_down, tn=256, mode="residual", extra=(h,),
                      next_gains=next_gains, name="ffn_down")

    xn, = rmsnorm(x, a_norm, [BF16], rows_per_step=tr)
    bcx = matmul(xn, [a_w_in], layer=0, tm=tm_big, tn=512, x_buffers=1, name="conv_in")
    g, conv_prompt, conv_sample = conv_gate(bcx, a_conv_w[0], state_conv[0], n_batch=n_batch, seq=seq,
                                            n_sample=n_sample, tc=256)
    h, h_scaled, h_ssq = matmul(g, [a_w_out], layer=0, tm=tm_big, tn=512, x_buffers=1, mode="residual",
                                extra=(x,), next_gains=ffn_norm[0:1], name="conv_out")
    h, kv_in, q_in, h_ssq = ffn(h, h_scaled, h_ssq, 0, jnp.stack([kv_norm, b_norm[0]]))

    k_p, v_p = matmul(kv_in, [w_k, w_v], tm=tm_p, tn=256, mode="rotary_plain", extra=rope_p, x_ssq=h_ssq,
                      name="kv_proj", n_row_blocks=rows_p // tm_p)
    k_s, v_s = matmul(kv_in, [w_k, w_v], tm=ROW_PAD, tn=256, mode="rotary_plain", extra=rope_s, x_ssq=h_ssq,
                      name="kv_proj_sample", row_block0=sample_blk, n_row_blocks=1)
    q = matmul(q_in, [w_q], layer=0, tm=tm_big, tn=512, x_buffers=1, mode="rotary", extra=rope_all,
               x_ssq=h_ssq, name="q_proj")

    kmean = paged_block_means(cache_k, page_table)
    q_s = q[rows_p:rows_p + n_sample].reshape(n_sample, n_heads, HEAD_DIM)
    sel = sample_topk(q_s, kmean.transpose(0, 2, 1, 3))[:, :, :MOBA_TOP_K].reshape(-1)
    k_s = k_s[:n_sample].reshape(n_sample, n_kv, 1, HEAD_DIM)
    v_s = v_s[:n_sample].reshape(n_sample, n_kv, 1, HEAD_DIM)
    attn_s = sample_attend(q_s.reshape(n_sample, n_heads, 1, HEAD_DIM), k_s, v_s, cache_k, cache_v,
                           page_table, sel)
    attn_s = jnp.concatenate([attn_s.reshape(n_sample, d_model),
                              jnp.zeros((ROW_PAD - n_sample, d_model), jnp.float32)], axis=0)
    attn = moba_attention(q, k_p, v_p, attn_s, n_batch=n_batch, seq=seq)

    h, h_scaled, h_ssq = matmul(attn, [w_o], layer=0, tm=tm_big, tn=512, x_buffers=1, mode="residual",
                                extra=(h,), next_gains=ffn_norm[1:2], name="attn_out")
    h = ffn(h, h_scaled, h_ssq, 1, None)
    y_p, = rmsnorm(h, final_norm[None, :], [jnp.float32], rows_per_step=tr_p, n_row_blocks=rows_p // tr_p)
    y_s, = rmsnorm(h, final_norm[None, :], [jnp.float32], rows_per_step=ROW_PAD, row_block0=sample_blk,
                   n_row_blocks=1)

    return (y_p.reshape(n_batch, seq, d_model), y_s[:n_sample].reshape(n_sample, 1, d_model),
            conv_prompt[None], conv_sample[None],
            k_p.reshape(n_batch, seq, n_kv, HEAD_DIM), v_p.reshape(n_batch, seq, n_kv, HEAD_DIM),
            k_s.reshape(n_sample, 1, n_kv, HEAD_DIM), v_s.reshape(n_sample, 1, n_kv, HEAD_DIM))
```

```python
import functools

import jax
import jax.numpy as jnp
from jax import lax
from jax.experimental import pallas as pl
from jax.experimental.pallas import tpu as pltpu

HEAD_DIM = 128
GROUP = 4
ROT_DIM = HEAD_DIM // 4
ROT_HALF = ROT_DIM // 2
ROPE_THETA = 500000.0
MOBA_BLOCK = 256
MOBA_TOP_K = 3
CONV_WIDTH = 3
RMS_EPS = 1e-6
NEG_INF = -1e30

LANES = 128
BF16_SUBLANES = 16
VMEM_LIMIT_BYTES = 58 * 1024 * 1024

ROW_PAD = 128
BF16 = jnp.bfloat16


def _row_tile(rows, target):
    return max(t for t in range(BF16_SUBLANES, target + 1, BF16_SUBLANES) if rows % t == 0)


def _cparams(semantics):
    return pltpu.CompilerParams(dimension_semantics=semantics, vmem_limit_bytes=VMEM_LIMIT_BYTES)


def _rmsnorm_kernel(x_ref, g_ref, *o_refs):
    x = x_ref[...]
    xn = x * lax.rsqrt(jnp.mean(x * x, axis=-1, keepdims=True) + RMS_EPS)
    for i, o_ref in enumerate(o_refs):
        o_ref[...] = (xn * g_ref[i:i + 1, :]).astype(o_ref.dtype)


def rmsnorm(x, gains, out_dtypes, *, rows_per_step, row_block0=0, n_row_blocks=None):
    rows, d = x.shape
    n_out = gains.shape[0]
    if n_row_blocks is None:
        n_row_blocks = rows // rows_per_step
    out_spec = pl.BlockSpec((rows_per_step, d), lambda r: (r, 0))
    return pl.pallas_call(
        _rmsnorm_kernel,
        grid=(n_row_blocks,),
        in_specs=[pl.BlockSpec((rows_per_step, d), lambda r: (r + row_block0, 0)),
                  pl.BlockSpec((n_out, d), lambda r: (0, 0))],
        out_specs=[out_spec] * n_out,
        out_shape=[jax.ShapeDtypeStruct((n_row_blocks * rows_per_step, d), dt) for dt in out_dtypes],
        compiler_params=_cparams(("parallel",)),
        name="rmsnorm",
    )(x, gains)


def _rotary_epilogue(y, cos, sin_signed):
    lane = lax.broadcasted_iota(jnp.int32, (y.shape[0], HEAD_DIM), 1)
    first_half = lane < ROT_HALF
    outs = []
    for h in range(y.shape[1] // HEAD_DIM):
        yh = y[:, h * HEAD_DIM:(h + 1) * HEAD_DIM]
        from_above = pltpu.roll(yh, HEAD_DIM - ROT_HALF, axis=1)
        from_below = pltpu.roll(yh, ROT_HALF, axis=1)
        partner = jnp.where(first_half, from_above, from_below)
        outs.append(yh * cos + partner * sin_signed)
    return outs[0] if len(outs) == 1 else jnp.concatenate(outs, axis=1)


def _matmul_kernel(*refs, n_w, mode, normed_x, n_gains, side_cast):
    refs = list(refs)
    x_ref = refs.pop(0)
    w_refs = [refs.pop(0) for _ in range(n_w)]
    ssq_in_ref = refs.pop(0) if normed_x else None
    res_ref = refs.pop(0) if mode == "residual" else None
    gains_ref = refs.pop(0) if n_gains else None
    rope_refs = [refs.pop(0) for _ in range(2)] if mode in ("rotary", "rotary_plain") else None
    side_in_ref = refs.pop(0) if side_cast else None
    o_refs = [refs.pop(0) for _ in range(2 if mode == "rotary_plain" else 1)]
    scaled_refs = [refs.pop(0) for _ in range(n_gains)]
    ssq_out_ref = refs.pop(0) if n_gains else None
    side_out_ref = refs.pop(0) if side_cast else None
    assert not refs
    i, j = pl.program_id(0), pl.program_id(1)

    x = x_ref[...]
    vals = [jnp.dot(x, w_ref[...].astype(x.dtype), preferred_element_type=jnp.float32) for w_ref in w_refs]
    if normed_x:
        inv_rms = lax.rsqrt(ssq_in_ref[:, 0:1] * (1.0 / x.shape[1]) + RMS_EPS)
        vals = [v * inv_rms for v in vals]

    if mode == "swiglu":
        g, u = vals
        ys = [(g * jax.nn.sigmoid(g)) * u]
    elif mode == "residual":
        ys = [res_ref[...] + vals[0]]
    elif mode == "rotary":
        ys = [_rotary_epilogue(vals[0], rope_refs[0][...], rope_refs[1][...])]
    elif mode == "rotary_plain":
        ys = [_rotary_epilogue(vals[0], rope_refs[0][...], rope_refs[1][...]), vals[1]]
    else:
        ys = vals
    for o_ref, y in zip(o_refs, ys):
        o_ref[...] = y.astype(o_ref.dtype)

    if n_gains:
        h = ys[0]
        for n, scaled_ref in enumerate(scaled_refs):
            scaled_ref[...] = (h * gains_ref[n:n + 1, :]).astype(scaled_ref.dtype)
        part = jnp.broadcast_to(jnp.sum(h * h, axis=1, keepdims=True), ssq_out_ref.shape)

        @pl.when(j == 0)
        def _():
            ssq_out_ref[...] = part

        @pl.when(j > 0)
        def _():
            ssq_out_ref[...] += part

    if side_cast:
        @pl.when(i == 0)
        def _():
            side_out_ref[...] = side_in_ref[...].astype(side_out_ref.dtype)


def matmul(x, ws, *, tm, tn, layer=None, mode="plain", extra=(), out_dtype=jnp.float32, name="matmul",
           row_block0=0, n_row_blocks=None, x_buffers=2, x_ssq=None, next_gains=None, side_cast=None):
    m, k_dim = x.shape
    n = ws[0].shape[-1]
    tn = min(tn, n)
    n_j = n // tn
    if n_row_blocks is None:
        n_row_blocks = m // tm
    rows_out = n_row_blocks * tm
    assert n % tn == 0 and (row_block0 + n_row_blocks) * tm <= m
    assert next_gains is None or mode == "residual"
    n_gains = 0 if next_gains is None else next_gains.shape[0]

    out_tile = pl.BlockSpec((tm, tn), lambda i, j: (i, j))

    def per_row(width):
        return pl.BlockSpec((tm, width), lambda i, j: (i, 0))

    x_mode = {} if x_buffers == 2 else {"pipeline_mode": pl.Buffered(x_buffers)}
    operands = [x]
    in_specs = [pl.BlockSpec((tm, k_dim), lambda i, j: (i + row_block0, 0), **x_mode)]
    operands += ws
    if layer is None:
        in_specs += [pl.BlockSpec((k_dim, tn), lambda i, j: (0, j)) for _ in ws]
    else:
        in_specs += [pl.BlockSpec((None, k_dim, tn), lambda i, j: (layer, 0, j)) for _ in ws]
    if x_ssq is not None:
        operands.append(x_ssq)
        in_specs.append(pl.BlockSpec((tm, LANES), lambda i, j: (i + row_block0, 0)))
    if mode == "residual":
        operands.append(extra[0])
        in_specs.append(out_tile)
    if n_gains:
        operands.append(next_gains)
        in_specs.append(pl.BlockSpec((n_gains, tn), lambda i, j: (0, j)))
    if mode in ("rotary", "rotary_plain"):
        operands += extra
        in_specs += [per_row(HEAD_DIM)] * 2

    n_main = 2 if mode == "rotary_plain" else 1
    out_specs = [out_tile] * n_main
    out_shapes = [jax.ShapeDtypeStruct((rows_out, n), out_dtype)] * n_main
    if n_gains:
        out_specs += [out_tile] * n_gains + [per_row(LANES)]
        out_shapes += [jax.ShapeDtypeStruct((rows_out, n), BF16)] * n_gains
        out_shapes.append(jax.ShapeDtypeStruct((rows_out, LANES), jnp.float32))
    if side_cast is not None:
        side_w, side_layer = side_cast
        _, side_rows, side_cols = side_w.shape
        slab = side_rows // n_j
        assert side_rows % n_j == 0 and slab % BF16_SUBLANES == 0

        def slab_index(i, j):
            return jnp.where(i == 0, j, n_j - 1)

        operands.append(side_w)
        in_specs.append(pl.BlockSpec((None, slab, side_cols), lambda i, j: (side_layer, slab_index(i, j), 0)))
        out_specs.append(pl.BlockSpec((slab, side_cols), lambda i, j: (slab_index(i, j), 0)))
        out_shapes.append(jax.ShapeDtypeStruct((side_rows, side_cols), BF16))

    outs = pl.pallas_call(
        functools.partial(_matmul_kernel, n_w=len(ws), mode=mode, normed_x=x_ssq is not None,
                          n_gains=n_gains, side_cast=side_cast is not None),
        grid=(n_row_blocks, n_j),
        in_specs=in_specs,
        out_specs=out_specs,
        out_shape=out_shapes,
        compiler_params=_cparams(("arbitrary" if side_cast is not None else "parallel",
                                  "arbitrary" if (n_gains or side_cast is not None) else "parallel")),
        name=name,
    )(*operands)
    return outs if len(outs) > 1 else outs[0]


def _conv_in_kernel(x_ref, wb_ref, wc_ref, wx_ref, st0_ref, st1_ref, cw_ref, g_ref, state_ref, us_ref,
                    carry_ref, *, seq, n_batch):
    i = pl.program_id(1)
    tm = x_ref.shape[0]
    halo = CONV_WIDTH - 1
    x = x_ref[...]
    b, c, x_in = [jnp.dot(x, w_ref[...].astype(x.dtype), preferred_element_type=jnp.float32)
                  for w_ref in (wb_ref, wc_ref, wx_ref)]
    u = c * x_in

    @pl.when(i == 0)
    def _():
        carry_ref[...] = jnp.zeros(carry_ref.shape, carry_ref.dtype)

    t = lax.broadcasted_iota(jnp.int32, u.shape, 0)
    u_prev1 = jnp.where(t == 0, carry_ref[1:2, :], pltpu.roll(u, 1, axis=0))
    u_prev2 = jnp.where(t == 0, carry_ref[0:1, :], jnp.where(t == 1, carry_ref[1:2, :], pltpu.roll(u, 2, axis=0)))
    start = (i * tm + seq - 1) // seq * seq - i * tm
    start = jnp.where(start + i * tm < n_batch * seq, start, tm)
    u_prev1 = jnp.where(t == start, 0.0, u_prev1)
    u_prev2 = jnp.where(t == start, 0.0, jnp.where(t == start + 1, 0.0, u_prev2))
    conv = cw_ref[0:1, :] * u_prev2 + cw_ref[1:2, :] * u_prev1 + cw_ref[2:3, :] * u
    g_ref[...] = (b * conv).astype(g_ref.dtype)
    carry_ref[0:halo, :] = u[tm - halo:, :]

    for n in range(n_batch):
        tile, row = divmod((n + 1) * seq - halo, tm)

        @pl.when(i == tile)
        def _(n=n, row=row):
            state_ref[n] = u[row:row + halo, :]

    @pl.when(i == pl.num_programs(1) - 1)
    def _():
        us = u[tm - ROW_PAD:, :]
        conv_s = cw_ref[0:1, :] * st0_ref[...] + cw_ref[1:2, :] * st1_ref[...] + cw_ref[2:3, :] * us
        g_ref[tm - ROW_PAD:, :] = (b[tm - ROW_PAD:, :] * conv_s).astype(g_ref.dtype)
        us_ref[...] = us


def conv_in(xn, w_in, conv_w, state_sample, *, n_batch, seq, n_sample, tm, tc):
    rows, d = xn.shape
    n_c = d // tc
    halo = CONV_WIDTH - 1
    assert rows % tm == 0 and tm <= seq and rows == n_batch * seq + ROW_PAD
    for n in range(1, n_batch + 1):
        assert (n * seq) % tm != tm - 1 and ((n * seq - halo) % tm) + halo <= tm
    pad = jnp.zeros((ROW_PAD - n_sample, d), jnp.float32)
    st0 = jnp.concatenate([state_sample[:, 0, :], pad], axis=0)
    st1 = jnp.concatenate([state_sample[:, 1, :], pad], axis=0)

    def w_spec(gate):
        return pl.BlockSpec((None, d, tc), lambda j, i: (0, 0, gate * n_c + j))

    sample_spec = pl.BlockSpec((ROW_PAD, tc), lambda j, i: (0, j))
    g, st_prompt, u_sample = pl.pallas_call(
        functools.partial(_conv_in_kernel, seq=seq, n_batch=n_batch),
        grid=(n_c, rows // tm),
        in_specs=[pl.BlockSpec((tm, d), lambda j, i: (i, 0)), w_spec(0), w_spec(1), w_spec(2),
                  sample_spec, sample_spec, pl.BlockSpec((CONV_WIDTH, tc), lambda j, i: (0, j))],
        out_specs=[pl.BlockSpec((tm, tc), lambda j, i: (i, j)),
                   pl.BlockSpec((n_batch, halo, tc), lambda j, i: (0, 0, j)),
                   sample_spec],
        out_shape=[jax.ShapeDtypeStruct((rows, d), BF16),
                   jax.ShapeDtypeStruct((n_batch, halo, d), jnp.float32),
                   jax.ShapeDtypeStruct((ROW_PAD, d), jnp.float32)],
        scratch_shapes=[pltpu.VMEM((8, tc), jnp.float32)],
        compiler_params=_cparams(("parallel", "arbitrary")),
        name="conv_in",
    )(xn, w_in, w_in, w_in, st0, st1, conv_w)
    st_sample = jnp.stack([state_sample[:, 1, :], u_sample[:n_sample]], axis=1)
    return g, st_prompt, st_sample


MOBA_KV_PER_STEP = 2


def _moba_kernel(q_ref, k_ref, v_ref, sample_ref, o_ref, *scratch, n_blocks, n_prompt_tiles):
    tile = pl.program_id(1)

    @pl.when(tile < n_prompt_tiles)
    def _():
        _moba_prompt_tile(q_ref, k_ref, v_ref, o_ref, *scratch, own=tile % n_blocks, n_blocks=n_blocks)

    @pl.when(tile == n_prompt_tiles)
    def _():
        o_ref[0:ROW_PAD, :] = sample_ref[...].astype(o_ref.dtype)


def _moba_prompt_tile(q_ref, k_ref, v_ref, o_ref, kbf_ref, vt_ref, kmean_ref, qs_ref, sel_ref,
                      s_ref, m_ref, l_ref, acc_ref, *, own, n_blocks):
    blk = MOBA_BLOCK
    n_kv_step = kbf_ref.shape[0]
    n_heads_step = n_kv_step * GROUP
    nt_dims = (((1,), (1,)), ((), ()))

    @pl.when(own == 0)
    def _():
        for c in range(n_kv_step):
            cols = slice(c * HEAD_DIM, (c + 1) * HEAD_DIM)
            kbf_ref[c] = k_ref[:, cols].astype(BF16)
            for n in range(n_blocks):
                rows = slice(n * blk, (n + 1) * blk)
                kmean_ref[c, n:n + 1, :] = jnp.mean(k_ref[rows, cols], axis=0, keepdims=True)
                vt_ref[c, n] = jnp.transpose(v_ref[rows, cols]).astype(BF16)

    for g in range(n_heads_step):
        q = q_ref[:, g * HEAD_DIM:(g + 1) * HEAD_DIM]
        gate_t = lax.dot_general(kmean_ref[g // GROUP], q, nt_dims, precision=lax.Precision.HIGHEST,
                                 preferred_element_type=jnp.float32)
        n_iota = lax.broadcasted_iota(jnp.int32, gate_t.shape, 0)
        beaten_by = jnp.zeros(gate_t.shape, jnp.int32)
        for n in range(n_blocks):
            row = gate_t[n:n + 1, :]
            beats = jnp.logical_or(row > gate_t, jnp.logical_and(row == gate_t, n < n_iota))
            beaten_by = beaten_by + jnp.where(beats, 1, 0) * (n < own).astype(jnp.int32)
        sel_t = jnp.logical_and(beaten_by < MOBA_TOP_K, n_iota < own)
        sel_ref[g] = jnp.where(sel_t, 1.0, 0.0)
        qs_ref[g] = (q * (HEAD_DIM ** -0.5)).astype(BF16)
    m_ref[...] = jnp.full(m_ref.shape, -jnp.inf, jnp.float32)
    l_ref[...] = jnp.zeros(l_ref.shape, jnp.float32)
    acc_ref[...] = jnp.zeros(acc_ref.shape, jnp.float32)

    def score_block(j, keep_of_head):
        key_rows = pl.ds(pl.multiple_of(j * blk, blk), blk)
        for g in range(n_heads_step):
            s_t = lax.dot_general(kbf_ref[g // GROUP, key_rows, :], qs_ref[g], nt_dims,
                                  preferred_element_type=jnp.float32)
            s_t = jnp.where(keep_of_head(g), s_t, NEG_INF)
            s_ref[g, j] = s_t
            m_ref[g:g + 1, :] = jnp.maximum(m_ref[g:g + 1, :], jnp.max(s_t, axis=0, keepdims=True))

    def score_past(j, carry):
        score_block(j, lambda g: sel_ref[g, pl.ds(j, 1), :] > 0.5)
        return carry

    lax.fori_loop(0, own, score_past, 0)
    k_pos = lax.broadcasted_iota(jnp.int32, (blk, blk), 0)
    q_pos = lax.broadcasted_iota(jnp.int32, (blk, blk), 1)
    causal = k_pos <= q_pos
    score_block(own, lambda g: causal)

    def weigh_block(j, carry):
        for g in range(n_heads_step):
            p_t = jnp.exp(s_ref[g, j] - m_ref[g:g + 1, :])
            l_ref[g:g + 1, :] += jnp.sum(p_t, axis=0, keepdims=True)
            acc_ref[g] += jnp.dot(vt_ref[g // GROUP, j], p_t.astype(BF16),
                                  preferred_element_type=jnp.float32)
        return carry

    lax.fori_loop(0, own + 1, weigh_block, 0)

    for g in range(n_heads_step):
        out_t = acc_ref[g] / l_ref[g:g + 1, :]
        o_ref[:, g * HEAD_DIM:(g + 1) * HEAD_DIM] = jnp.transpose(out_t).astype(o_ref.dtype)


def moba_attention(q, k, v, attn_sample, *, n_batch, seq):
    rows, d = q.shape
    n_kv = k.shape[1] // HEAD_DIM
    n_blocks = seq // MOBA_BLOCK
    n_prompt_tiles = n_batch * n_blocks
    kvs = MOBA_KV_PER_STEP if n_kv % MOBA_KV_PER_STEP == 0 else 1
    hs = kvs * GROUP
    q_spec = pl.BlockSpec((MOBA_BLOCK, hs * HEAD_DIM), lambda h, t: (jnp.minimum(t, n_prompt_tiles - 1), h))
    kv_spec = pl.BlockSpec((seq, kvs * HEAD_DIM), lambda h, t: (jnp.minimum(t // n_blocks, n_batch - 1), h))
    return pl.pallas_call(
        functools.partial(_moba_kernel, n_blocks=n_blocks, n_prompt_tiles=n_prompt_tiles),
        grid=(n_kv // kvs, n_prompt_tiles + 1),
        in_specs=[q_spec, kv_spec, kv_spec, pl.BlockSpec((ROW_PAD, hs * HEAD_DIM), lambda h, t: (0, h))],
        out_specs=pl.BlockSpec((MOBA_BLOCK, hs * HEAD_DIM), lambda h, t: (t, h)),
        out_shape=jax.ShapeDtypeStruct((rows, d), BF16),
        scratch_shapes=[pltpu.VMEM((kvs, seq, HEAD_DIM), BF16),
                        pltpu.VMEM((kvs, n_blocks, HEAD_DIM, MOBA_BLOCK), BF16),
                        pltpu.VMEM((kvs, n_blocks, HEAD_DIM), jnp.float32),
                        pltpu.VMEM((hs, MOBA_BLOCK, HEAD_DIM), BF16),
                        pltpu.VMEM((hs, n_blocks, MOBA_BLOCK), jnp.float32),
                        pltpu.VMEM((hs, n_blocks, MOBA_BLOCK, MOBA_BLOCK), jnp.float32),
                        pltpu.VMEM((hs, MOBA_BLOCK), jnp.float32),
                        pltpu.VMEM((hs, MOBA_BLOCK), jnp.float32),
                        pltpu.VMEM((hs, HEAD_DIM, MOBA_BLOCK), jnp.float32)],
        compiler_params=_cparams(("parallel", "arbitrary")),
        name="moba_attention",
    )(q, k, v, attn_sample)


PAGES_PER_STEP = 16
GATHER_SLOTS = 8


def _page_mean_kernel(pt_ref, *refs, pages_per_block):
    del pt_ref
    page_refs, o_ref = refs[:-1], refs[-1]
    n_out = len(page_refs) // pages_per_block
    page_rows = page_refs[0].shape[0]
    for n in range(n_out):
        total = jnp.sum(page_refs[n * pages_per_block][...], axis=0)
        for r in range(1, pages_per_block):
            total = total + jnp.sum(page_refs[n * pages_per_block + r][...], axis=0)
        o_ref[n] = total / (pages_per_block * page_rows)


def paged_block_means(cache_k, page_table):
    _, page, n_kv, _ = cache_k.shape
    n_seq, n_pages = page_table.shape
    pages_per_block = MOBA_BLOCK // page
    blocks_per_step = PAGES_PER_STEP // pages_per_block
    n_steps = n_pages // PAGES_PER_STEP
    in_specs = [pl.BlockSpec((None, page, n_kv, HEAD_DIM),
                             lambda b, s, pt, r=r: (pt[b, s * PAGES_PER_STEP + r], 0, 0, 0))
                for r in range(PAGES_PER_STEP)]
    return pl.pallas_call(
        functools.partial(_page_mean_kernel, pages_per_block=pages_per_block),
        grid_spec=pltpu.PrefetchScalarGridSpec(
            num_scalar_prefetch=1,
            grid=(n_seq, n_steps),
            in_specs=in_specs,
            out_specs=pl.BlockSpec((None, blocks_per_step, n_kv, HEAD_DIM), lambda b, s, pt: (b, s, 0, 0))),
        out_shape=jax.ShapeDtypeStruct((n_seq, n_pages // pages_per_block, n_kv, HEAD_DIM), jnp.float32),
        compiler_params=_cparams(("parallel", "arbitrary")),
        name="page_means",
    )(page_table, *([cache_k] * PAGES_PER_STEP))


def _sample_topk_kernel(q_ref, kmean_ref, sel_ref):
    q = q_ref[...]
    n_heads = q.shape[0]
    n_kv, n_blocks, _ = kmean_ref.shape
    head_kv = lax.broadcasted_iota(jnp.int32, (n_heads, n_blocks), 0) // GROUP
    gate = jnp.zeros((n_heads, n_blocks), jnp.float32)
    for kv in range(n_kv):
        g = lax.dot_general(q, kmean_ref[kv], (((1,), (1,)), ((), ())),
                            precision=lax.Precision.HIGHEST, preferred_element_type=jnp.float32)
        gate = jnp.where(head_kv == kv, g, gate)
    blk = lax.broadcasted_iota(jnp.int32, gate.shape, 1).astype(jnp.float32)
    lane = lax.broadcasted_iota(jnp.int32, sel_ref.shape, 1)
    sel = jnp.zeros(sel_ref.shape, jnp.float32)
    for s in range(MOBA_TOP_K):
        best = jnp.max(gate, axis=-1, keepdims=True)
        idx = jnp.min(jnp.where(gate == best, blk, float(n_blocks)), axis=-1, keepdims=True)
        sel = jnp.where(lane == s, idx, sel)
        gate = jnp.where(blk == idx, -jnp.inf, gate)
    sel_ref[...] = sel.astype(jnp.int32)


def sample_topk(q_s, kmean_t):
    n_seq, n_heads, _ = q_s.shape
    _, n_kv, n_blocks, _ = kmean_t.shape
    return pl.pallas_call(
        _sample_topk_kernel,
        grid=(n_seq,),
        in_specs=[pl.BlockSpec((None, n_heads, HEAD_DIM), lambda b: (b, 0, 0)),
                  pl.BlockSpec((None, n_kv, n_blocks, HEAD_DIM), lambda b: (b, 0, 0, 0))],
        out_specs=pl.BlockSpec((None, n_heads, LANES), lambda b: (b, 0, 0)),
        out_shape=jax.ShapeDtypeStruct((n_seq, n_heads, LANES), jnp.int32),
        compiler_params=_cparams(("parallel",)),
        name="sample_topk",
    )(q_s, kmean_t)


def _sample_attend_kernel(pt_ref, sel_ref, q_ref, knew_ref, vnew_ref, ck_hbm, cv_hbm, o_ref,
                          kbuf, vbuf, sems, *, pages_per_block):
    n_seq, n_heads, _, _ = q_ref.shape
    n_pages_sel = kbuf.shape[1]
    n_items = n_seq * n_heads

    def page_copies(item, slot):
        b = item // n_heads
        kv = (item % n_heads) // GROUP
        copies = []
        for idx in range(n_pages_sel):
            blk = sel_ref[item * MOBA_TOP_K + idx // pages_per_block]
            pg = pt_ref[b, blk * pages_per_block + idx % pages_per_block]
            copies.append(pltpu.make_async_copy(ck_hbm.at[pg, :, kv, :], kbuf.at[slot, idx], sems.at[0, slot, idx]))
            copies.append(pltpu.make_async_copy(cv_hbm.at[pg, :, kv, :], vbuf.at[slot, idx], sems.at[1, slot, idx]))
        return copies

    n_slots = kbuf.shape[0]
    lookahead = n_slots - 1
    for item in range(min(lookahead, n_items)):
        for c in page_copies(item, item % n_slots):
            c.start()

    def step(item, carry):
        slot = item % n_slots

        @pl.when(item + lookahead < n_items)
        def _():
            for c in page_copies(item + lookahead, (item + lookahead) % n_slots):
                c.start()

        for c in page_copies(item, slot):
            c.wait()
        b = item // n_heads
        h = item % n_heads
        o_ref[b, h] = _sample_attend_one(q_ref[b, h], knew_ref[b, h // GROUP], vnew_ref[b, h // GROUP],
                                         [kbuf[slot, idx] for idx in range(n_pages_sel)],
                                         [vbuf[slot, idx] for idx in range(n_pages_sel)])
        return carry

    lax.fori_loop(0, n_items, step, 0)


def _sample_attend_one(q, k_new, v_new, k_pages, v_pages):
    q_scaled = q * (HEAD_DIM ** -0.5)
    qs = q_scaled.astype(BF16)
    qs8 = jnp.broadcast_to(q_scaled, (8, HEAD_DIM)).astype(BF16)
    scores = [lax.dot_general(qs8, kp.astype(BF16), (((1,), (1,)), ((), ())),
                              preferred_element_type=jnp.float32) for kp in k_pages]
    knew = k_new.astype(BF16).astype(jnp.float32)
    s_own = jnp.sum(qs.astype(jnp.float32) * knew, axis=-1, keepdims=True)
    m = s_own
    for s in scores:
        m = jnp.maximum(m, jnp.max(s[0:1, :], axis=-1, keepdims=True))
    p_own = jnp.exp(s_own - m)
    denom = p_own
    ps = []
    for s in scores:
        p = jnp.exp(s - m)
        ps.append(p)
        denom = denom + jnp.sum(p[0:1, :], axis=-1, keepdims=True)
    inv = 1.0 / denom
    vnew = v_new.astype(BF16).astype(jnp.float32)
    out = (p_own * inv).astype(BF16).astype(jnp.float32) * vnew
    for p, vp in zip(ps, v_pages):
        pv = jnp.dot((p * inv).astype(BF16), vp.astype(BF16), preferred_element_type=jnp.float32)
        out = out + pv[0:1, :]
    return out


def sample_attend(q_s4, k_new4, v_new4, cache_k, cache_v, page_table, sel_flat):
    page = cache_k.shape[1]
    pages_per_block = MOBA_BLOCK // page
    n_pages_sel = MOBA_TOP_K * pages_per_block

    def whole(a):
        return pl.BlockSpec(a.shape, lambda i, pt, sel: (0,) * a.ndim)

    return pl.pallas_call(
        functools.partial(_sample_attend_kernel, pages_per_block=pages_per_block),
        grid_spec=pltpu.PrefetchScalarGridSpec(
            num_scalar_prefetch=2,
            grid=(1,),
            in_specs=[whole(q_s4), whole(k_new4), whole(v_new4),
                      pl.BlockSpec(memory_space=pl.ANY), pl.BlockSpec(memory_space=pl.ANY)],
            out_specs=whole(q_s4),
            scratch_shapes=[pltpu.VMEM((GATHER_SLOTS, n_pages_sel, page, HEAD_DIM), jnp.float32),
                            pltpu.VMEM((GATHER_SLOTS, n_pages_sel, page, HEAD_DIM), jnp.float32),
                            pltpu.SemaphoreType.DMA((2, GATHER_SLOTS, n_pages_sel))]),
        out_shape=jax.ShapeDtypeStruct(q_s4.shape, jnp.float32),
        compiler_params=_cparams(("arbitrary",)),
        name="sample_attend",
    )(page_table, sel_flat, q_s4, k_new4, v_new4, cache_k, cache_v)


def _rope_tables(positions):
    inv_freq = ROPE_THETA ** (-jnp.arange(ROT_HALF, dtype=jnp.float32) * 2.0 / ROT_DIM)
    ang = positions.astype(jnp.float32)[:, None] * inv_freq[None, :]
    cos, sin = jnp.cos(ang), jnp.sin(ang)
    rest = HEAD_DIM - ROT_DIM
    ones = jnp.ones((positions.shape[0], rest), jnp.float32)
    zeros = jnp.zeros((positions.shape[0], rest), jnp.float32)
    return (jnp.concatenate([cos, cos, ones], axis=1), jnp.concatenate([-sin, sin, zeros], axis=1))


def kernel(x_prompt, x_sample, state_conv, cache_k, cache_v, page_table, a_norm, a_w_in, a_conv_w, a_w_out,
           kv_norm, w_k, w_v, b_norm, w_q, w_o, ffn_norm, w_gate, w_up, w_down, final_norm):
    n_batch, seq, d_model = x_prompt.shape
    n_sample = x_sample.shape[0]
    assert x_sample.shape[1] == 1 and n_sample <= ROW_PAD
    assert a_w_in.shape[0] == 1 and w_q.shape[0] == 1, "one short-conv layer then one MoBA layer"
    _, page, n_kv, _ = cache_k.shape
    n_heads = d_model // HEAD_DIM
    past_len = page_table.shape[1] * page
    assert past_len % MOBA_BLOCK == 0 and past_len // MOBA_BLOCK >= MOBA_TOP_K
    rows_p = n_batch * seq
    rows = rows_p + ROW_PAD
    sample_blk = rows_p // ROW_PAD

    tr, tm_down, tm_big = (_row_tile(rows, t) for t in (320, 832, 1664))
    tr_p, tm_p = _row_tile(rows_p, 256), _row_tile(rows_p, 1024)

    x = jnp.concatenate([x_prompt.reshape(rows_p, d_model), x_sample.reshape(n_sample, d_model),
                         jnp.zeros((ROW_PAD - n_sample, d_model), x_prompt.dtype)], axis=0)
    pos_p = jnp.tile(jnp.arange(seq, dtype=jnp.int32), n_batch)
    pos_s = jnp.concatenate([jnp.full((n_sample,), past_len, jnp.int32),
                             jnp.zeros((ROW_PAD - n_sample,), jnp.int32)])
    rope_p, rope_s = _rope_tables(pos_p), _rope_tables(pos_s)
    rope_all = tuple(jnp.concatenate([p, s], axis=0) for p, s in zip(rope_p, rope_s))

    def ffn(h, h_scaled, h_ssq, layer, next_gains):
        act, w_down_bf16 = matmul(h_scaled, [w_gate, w_up], layer=layer, tm=tm_big, tn=256, x_buffers=1,
                                  mode="swiglu", out_dtype=BF16, x_ssq=h_ssq, side_cast=(w_down, layer),
                                  name="ffn_gate_up")
        return matmul(act, [w_down_bf16], tm=tm_down, tn=256, mode="residual", extra=(h,),
                      next_gains=next_gains, name="ffn_down")

    xn, = rmsnorm(x, a_norm, [BF16], rows_per_step=tr)
    g, conv_prompt, conv_sample = conv_in(xn, a_w_in, a_conv_w[0], state_conv[0], n_batch=n_batch, seq=seq,
                                          n_sample=n_sample, tm=_row_tile(rows, 1040), tc=256)
    h, h_scaled, h_ssq = matmul(g, [a_w_out], layer=0, tm=tm_big, tn=512, x_buffers=1, mode="residual",
                                extra=(x,), next_gains=ffn_norm[0:1], name="conv_out")
    h, kv_in, q_in, h_ssq = ffn(h, h_scaled, h_ssq, 0, jnp.stack([kv_norm, b_norm[0]]))

    k_p, v_p = matmul(kv_in, [w_k, w_v], tm=tm_p, tn=256, mode="rotary_plain", extra=rope_p, x_ssq=h_ssq,
                      name="kv_proj", n_row_blocks=rows_p // tm_p)
    k_s, v_s = matmul(kv_in, [w_k, w_v], tm=ROW_PAD, tn=256, mode="rotary_plain", extra=rope_s, x_ssq=h_ssq,
                      name="kv_proj_sample", row_block0=sample_blk, n_row_blocks=1)
    q = matmul(q_in, [w_q], layer=0, tm=tm_big, tn=512, x_buffers=1, mode="rotary", extra=rope_all,
               x_ssq=h_ssq, name="q_proj")

    kmean = paged_block_means(cache_k, page_table)
    q_s = q[rows_p:rows_p + n_sample].reshape(n_sample, n_heads, HEAD_DIM)
    sel = sample_topk(q_s, kmean.transpose(0, 2, 1, 3))[:, :, :MOBA_TOP_K].reshape(-1)
    k_s = k_s[:n_sample].reshape(n_sample, n_kv, 1, HEAD_DIM)
    v_s = v_s[:n_sample].reshape(n_sample, n_kv, 1, HEAD_DIM)
    attn_s = sample_attend(q_s.reshape(n_sample, n_heads, 1, HEAD_DIM), k_s, v_s, cache_k, cache_v,
                           page_table, sel)
    attn_s = jnp.concatenate([attn_s.reshape(n_sample, d_model),
                              jnp.zeros((ROW_PAD - n_sample, d_model), jnp.float32)], axis=0)
    attn = moba_attention(q, k_p, v_p, attn_s, n_batch=n_batch, seq=seq)

    h, h_scaled, h_ssq = matmul(attn, [w_o], layer=0, tm=tm_big, tn=512, x_buffers=1, mode="residual",
                                extra=(h,), next_gains=ffn_norm[1:2], name="attn_out")
    h = ffn(h, h_scaled, h_ssq, 1, None)
    y_p, = rmsnorm(h, final_norm[None, :], [jnp.float32], rows_per_step=tr_p, n_row_blocks=rows_p // tr_p)
    y_s, = rmsnorm(h, final_norm[None, :], [jnp.float32], rows_per_step=ROW_PAD, row_block0=sample_blk,
                   n_row_blocks=1)

    return (y_p.reshape(n_batch, seq, d_model), y_s[:n_sample].reshape(n_sample, 1, d_model),
            conv_prompt[None], conv_sample[None],
            k_p.reshape(n_batch, seq, n_kv, HEAD_DIM), v_p.reshape(n_batch, seq, n_kv, HEAD_DIM),
            k_s.reshape(n_sample, 1, n_kv, HEAD_DIM), v_s.reshape(n_sample, 1, n_kv, HEAD_DIM))
```

```python
import functools

import jax
import jax.numpy as jnp
from jax import lax
from jax.experimental import pallas as pl
from jax.experimental.pallas import tpu as pltpu

HEAD_DIM = 128
GROUP = 4
ROT_DIM = HEAD_DIM // 4
ROT_HALF = ROT_DIM // 2
ROPE_THETA = 500000.0
MOBA_BLOCK = 256
MOBA_TOP_K = 3
CONV_WIDTH = 3
RMS_EPS = 1e-6
NEG_INF = -1e30

LANES = 128
BF16_SUBLANES = 16
VMEM_LIMIT_BYTES = 58 * 1024 * 1024

ROW_PAD = 128
BF16 = jnp.bfloat16


def _row_tile(rows, target):
    return max(t for t in range(BF16_SUBLANES, target + 1, BF16_SUBLANES) if rows % t == 0)


def _cparams(semantics):
    return pltpu.CompilerParams(dimension_semantics=semantics, vmem_limit_bytes=VMEM_LIMIT_BYTES)


def _rmsnorm_kernel(x_ref, g_ref, *o_refs):
    x = x_ref[...]
    xn = x * lax.rsqrt(jnp.mean(x * x, axis=-1, keepdims=True) + RMS_EPS)
    for i, o_ref in enumerate(o_refs):
        o_ref[...] = (xn * g_ref[i:i + 1, :]).astype(o_ref.dtype)


def rmsnorm(x, gains, out_dtypes, *, rows_per_step, row_block0=0, n_row_blocks=None):
    rows, d = x.shape
    n_out = gains.shape[0]
    if n_row_blocks is None:
        n_row_blocks = rows // rows_per_step
    out_spec = pl.BlockSpec((rows_per_step, d), lambda r: (r, 0))
    return pl.pallas_call(
        _rmsnorm_kernel,
        grid=(n_row_blocks,),
        in_specs=[pl.BlockSpec((rows_per_step, d), lambda r: (r + row_block0, 0)),
                  pl.BlockSpec((n_out, d), lambda r: (0, 0))],
        out_specs=[out_spec] * n_out,
        out_shape=[jax.ShapeDtypeStruct((n_row_blocks * rows_per_step, d), dt) for dt in out_dtypes],
        compiler_params=_cparams(("parallel",)),
        name="rmsnorm",
    )(x, gains)


def _rmsnorm_in_kernel(xp_ref, xs_ref, g_ref, xn_ref, xc_ref, *, n_prompt_steps):
    step = pl.program_id(0)

    def emit(x, rows):
        xn = x * lax.rsqrt(jnp.mean(x * x, axis=-1, keepdims=True) + RMS_EPS)
        xn_ref[rows, :] = (xn * g_ref[...]).astype(xn_ref.dtype)
        xc_ref[rows, :] = x

    @pl.when(step < n_prompt_steps)
    def _():
        emit(xp_ref[...], slice(None))

    @pl.when(step == n_prompt_steps)
    def _():
        emit(xs_ref[...], slice(0, ROW_PAD))


def rmsnorm_in(x_prompt2d, x_sample_pad, gain, *, rows_per_step):
    rows_p, d = x_prompt2d.shape
    n_prompt_steps = rows_p // rows_per_step
    assert rows_p % rows_per_step == 0 and ROW_PAD <= rows_per_step
    rows = rows_p + ROW_PAD
    out_spec = pl.BlockSpec((rows_per_step, d), lambda r: (r, 0))
    return pl.pallas_call(
        functools.partial(_rmsnorm_in_kernel, n_prompt_steps=n_prompt_steps),
        grid=(n_prompt_steps + 1,),
        in_specs=[pl.BlockSpec((rows_per_step, d), lambda r: (jnp.minimum(r, n_prompt_steps - 1), 0)),
                  pl.BlockSpec((ROW_PAD, d), lambda r: (0, 0)),
                  pl.BlockSpec((1, d), lambda r: (0, 0))],
        out_specs=[out_spec, out_spec],
        out_shape=[jax.ShapeDtypeStruct((rows, d), BF16), jax.ShapeDtypeStruct((rows, d), jnp.float32)],
        compiler_params=_cparams(("arbitrary",)),
        name="rmsnorm_in",
    )(x_prompt2d, x_sample_pad, gain)


def _rotary_epilogue(y, cos, sin_signed):
    lane = lax.broadcasted_iota(jnp.int32, (y.shape[0], HEAD_DIM), 1)
    first_half = lane < ROT_HALF
    outs = []
    for h in range(y.shape[1] // HEAD_DIM):
        yh = y[:, h * HEAD_DIM:(h + 1) * HEAD_DIM]
        from_above = pltpu.roll(yh, HEAD_DIM - ROT_HALF, axis=1)
        from_below = pltpu.roll(yh, ROT_HALF, axis=1)
        partner = jnp.where(first_half, from_above, from_below)
        outs.append(yh * cos + partner * sin_signed)
    return outs[0] if len(outs) == 1 else jnp.concatenate(outs, axis=1)


def _matmul_kernel(*refs, n_w, mode, normed_x, n_gains, side_cast):
    refs = list(refs)
    x_ref = refs.pop(0)
    w_refs = [refs.pop(0) for _ in range(n_w)]
    ssq_in_ref = refs.pop(0) if normed_x else None
    res_ref = refs.pop(0) if mode == "residual" else None
    gains_ref = refs.pop(0) if n_gains else None
    rope_refs = [refs.pop(0) for _ in range(2)] if mode in ("rotary", "rotary_plain") else None
    side_in_ref = refs.pop(0) if side_cast else None
    o_refs = [refs.pop(0) for _ in range(2 if mode == "rotary_plain" else 1)]
    scaled_refs = [refs.pop(0) for _ in range(n_gains)]
    ssq_out_ref = refs.pop(0) if n_gains else None
    side_out_ref = refs.pop(0) if side_cast else None
    assert not refs
    i, j = pl.program_id(0), pl.program_id(1)

    x = x_ref[...]
    vals = [jnp.dot(x, w_ref[...].astype(x.dtype), preferred_element_type=jnp.float32) for w_ref in w_refs]
    if normed_x:
        inv_rms = lax.rsqrt(ssq_in_ref[...] * (1.0 / x.shape[1]) + RMS_EPS)
        inv_rms = jnp.concatenate([inv_rms] * (vals[0].shape[1] // LANES), axis=1)
        vals = [v * inv_rms for v in vals]

    if mode == "swiglu":
        g, u = vals
        ys = [(g * jax.nn.sigmoid(g)) * u]
    elif mode == "residual":
        ys = [res_ref[...] + vals[0]]
    elif mode == "rotary":
        ys = [_rotary_epilogue(vals[0], rope_refs[0][...], rope_refs[1][...])]
    elif mode == "rotary_plain":
        ys = [_rotary_epilogue(vals[0], rope_refs[0][...], rope_refs[1][...]), vals[1]]
    else:
        ys = vals
    for o_ref, y in zip(o_refs, ys):
        o_ref[...] = y.astype(o_ref.dtype)

    if n_gains:
        h = ys[0]
        for n, scaled_ref in enumerate(scaled_refs):
            scaled_ref[...] = (h * gains_ref[n:n + 1, :]).astype(scaled_ref.dtype)
        part = jnp.broadcast_to(jnp.sum(h * h, axis=1, keepdims=True), ssq_out_ref.shape)

        @pl.when(j == 0)
        def _():
            ssq_out_ref[...] = part

        @pl.when(j > 0)
        def _():
            ssq_out_ref[...] += part

    if side_cast:
        @pl.when(i == 0)
        def _():
            side_out_ref[...] = side_in_ref[...].astype(side_out_ref.dtype)


def matmul(x, ws, *, tm, tn, layer=None, mode="plain", extra=(), out_dtype=jnp.float32, name="matmul",
           row_block0=0, n_row_blocks=None, x_buffers=2, x_ssq=None, next_gains=None, side_cast=None):
    m, k_dim = x.shape
    n = ws[0].shape[-1]
    tn = min(tn, n)
    n_j = n // tn
    if n_row_blocks is None:
        n_row_blocks = m // tm
    rows_out = n_row_blocks * tm
    assert n % tn == 0 and (row_block0 + n_row_blocks) * tm <= m
    assert next_gains is None or mode == "residual"
    n_gains = 0 if next_gains is None else next_gains.shape[0]

    out_tile = pl.BlockSpec((tm, tn), lambda i, j: (i, j))

    def per_row(width):
        return pl.BlockSpec((tm, width), lambda i, j: (i, 0))

    x_mode = {} if x_buffers == 2 else {"pipeline_mode": pl.Buffered(x_buffers)}
    operands = [x]
    in_specs = [pl.BlockSpec((tm, k_dim), lambda i, j: (i + row_block0, 0), **x_mode)]
    operands += ws
    if layer is None:
        in_specs += [pl.BlockSpec((k_dim, tn), lambda i, j: (0, j)) for _ in ws]
    else:
        in_specs += [pl.BlockSpec((None, k_dim, tn), lambda i, j: (layer, 0, j)) for _ in ws]
    if x_ssq is not None:
        operands.append(x_ssq)
        in_specs.append(pl.BlockSpec((tm, LANES), lambda i, j: (i + row_block0, 0)))
    if mode == "residual":
        operands.append(extra[0])
        in_specs.append(out_tile)
    if n_gains:
        operands.append(next_gains)
        in_specs.append(pl.BlockSpec((n_gains, tn), lambda i, j: (0, j)))
    if mode in ("rotary", "rotary_plain"):
        operands += extra
        in_specs += [per_row(HEAD_DIM)] * 2

    n_main = 2 if mode == "rotary_plain" else 1
    out_specs = [out_tile] * n_main
    out_shapes = [jax.ShapeDtypeStruct((rows_out, n), out_dtype)] * n_main
    if n_gains:
        out_specs += [out_tile] * n_gains + [per_row(LANES)]
        out_shapes += [jax.ShapeDtypeStruct((rows_out, n), BF16)] * n_gains
        out_shapes.append(jax.ShapeDtypeStruct((rows_out, LANES), jnp.float32))
    if side_cast is not None:
        side_w, side_layer = side_cast
        _, side_rows, side_cols = side_w.shape
        slab = side_rows // n_j
        assert side_rows % n_j == 0 and slab % BF16_SUBLANES == 0

        def slab_index(i, j):
            return jnp.where(i == 0, j, n_j - 1)

        operands.append(side_w)
        in_specs.append(pl.BlockSpec((None, slab, side_cols), lambda i, j: (side_layer, slab_index(i, j), 0)))
        out_specs.append(pl.BlockSpec((slab, side_cols), lambda i, j: (slab_index(i, j), 0)))
        out_shapes.append(jax.ShapeDtypeStruct((side_rows, side_cols), BF16))

    outs = pl.pallas_call(
        functools.partial(_matmul_kernel, n_w=len(ws), mode=mode, normed_x=x_ssq is not None,
                          n_gains=n_gains, side_cast=side_cast is not None),
        grid=(n_row_blocks, n_j),
        in_specs=in_specs,
        out_specs=out_specs,
        out_shape=out_shapes,
        compiler_params=_cparams(("arbitrary" if side_cast is not None else "parallel",
                                  "arbitrary" if (n_gains or side_cast is not None) else "parallel")),
        name=name,
    )(*operands)
    return outs if len(outs) > 1 else outs[0]


def _conv_in_kernel(x_ref, wb_ref, wc_ref, wx_ref, st0_ref, st1_ref, cw_ref, g_ref, state_ref, us_ref,
                    carry_ref, *, seq, n_batch):
    i = pl.program_id(1)
    tm = x_ref.shape[0]
    halo = CONV_WIDTH - 1
    x = x_ref[...]
    b, c, x_in = [jnp.dot(x, w_ref[...].astype(x.dtype), preferred_element_type=jnp.float32)
                  for w_ref in (wb_ref, wc_ref, wx_ref)]
    u = c * x_in

    @pl.when(i == 0)
    def _():
        carry_ref[...] = jnp.zeros(carry_ref.shape, carry_ref.dtype)

    t = lax.broadcasted_iota(jnp.int32, u.shape, 0)
    u_prev1 = jnp.where(t == 0, carry_ref[1:2, :], pltpu.roll(u, 1, axis=0))
    u_prev2 = jnp.where(t == 0, carry_ref[0:1, :], jnp.where(t == 1, carry_ref[1:2, :], pltpu.roll(u, 2, axis=0)))
    start = (i * tm + seq - 1) // seq * seq - i * tm
    start = jnp.where(start + i * tm < n_batch * seq, start, tm)
    u_prev1 = jnp.where(t == start, 0.0, u_prev1)
    u_prev2 = jnp.where(t == start, 0.0, jnp.where(t == start + 1, 0.0, u_prev2))
    conv = cw_ref[0:1, :] * u_prev2 + cw_ref[1:2, :] * u_prev1 + cw_ref[2:3, :] * u
    g_ref[...] = (b * conv).astype(g_ref.dtype)
    carry_ref[0:halo, :] = u[tm - halo:, :]

    for n in range(n_batch):
        tile, row = divmod((n + 1) * seq - halo, tm)

        @pl.when(i == tile)
        def _(n=n, row=row):
            state_ref[n] = u[row:row + halo, :]

    @pl.when(i == pl.num_programs(1) - 1)
    def _():
        us = u[tm - ROW_PAD:, :]
        conv_s = cw_ref[0:1, :] * st0_ref[...] + cw_ref[1:2, :] * st1_ref[...] + cw_ref[2:3, :] * us
        g_ref[tm - ROW_PAD:, :] = (b[tm - ROW_PAD:, :] * conv_s).astype(g_ref.dtype)
        us_ref[...] = us


def conv_in(xn, w_in, conv_w, state_sample, *, n_batch, seq, n_sample, tm, tc):
    rows, d = xn.shape
    n_c = d // tc
    halo = CONV_WIDTH - 1
    assert rows % tm == 0 and tm <= seq and rows == n_batch * seq + ROW_PAD
    for n in range(1, n_batch + 1):
        assert (n * seq) % tm != tm - 1 and ((n * seq - halo) % tm) + halo <= tm
    pad = jnp.zeros((ROW_PAD - n_sample, d), jnp.float32)
    st0 = jnp.concatenate([state_sample[:, 0, :], pad], axis=0)
    st1 = jnp.concatenate([state_sample[:, 1, :], pad], axis=0)

    def w_spec(gate):
        return pl.BlockSpec((None, d, tc), lambda j, i: (0, 0, gate * n_c + j))

    sample_spec = pl.BlockSpec((ROW_PAD, tc), lambda j, i: (0, j))
    g, st_prompt, u_sample = pl.pallas_call(
        functools.partial(_conv_in_kernel, seq=seq, n_batch=n_batch),
        grid=(n_c, rows // tm),
        in_specs=[pl.BlockSpec((tm, d), lambda j, i: (i, 0)), w_spec(0), w_spec(1), w_spec(2),
                  sample_spec, sample_spec, pl.BlockSpec((CONV_WIDTH, tc), lambda j, i: (0, j))],
        out_specs=[pl.BlockSpec((tm, tc), lambda j, i: (i, j)),
                   pl.BlockSpec((n_batch, halo, tc), lambda j, i: (0, 0, j)),
                   sample_spec],
        out_shape=[jax.ShapeDtypeStruct((rows, d), BF16),
                   jax.ShapeDtypeStruct((n_batch, halo, d), jnp.float32),
                   jax.ShapeDtypeStruct((ROW_PAD, d), jnp.float32)],
        scratch_shapes=[pltpu.VMEM((8, tc), jnp.float32)],
        compiler_params=_cparams(("parallel", "arbitrary")),
        name="conv_in",
    )(xn, w_in, w_in, w_in, st0, st1, conv_w)
    st_sample = jnp.stack([state_sample[:, 1, :], u_sample[:n_sample]], axis=1)
    return g, st_prompt, st_sample


MOBA_KV_PER_STEP = 2


def _moba_kernel(q_ref, k_ref, v_ref, sample_ref, o_ref, *scratch, n_blocks, n_prompt_tiles):
    tile = pl.program_id(1)

    @pl.when(tile < n_prompt_tiles)
    def _():
        _moba_prompt_tile(q_ref, k_ref, v_ref, o_ref, *scratch, own=tile % n_blocks, n_blocks=n_blocks)

    @pl.when(tile == n_prompt_tiles)
    def _():
        o_ref[0:ROW_PAD, :] = sample_ref[...].astype(o_ref.dtype)


def _moba_prompt_tile(q_ref, k_ref, v_ref, o_ref, kbf_ref, vt_ref, kmean_ref, qs_ref, sel_ref,
                      s_ref, m_ref, l_ref, acc_ref, *, own, n_blocks):
    blk = MOBA_BLOCK
    n_kv_step = kbf_ref.shape[0]
    n_heads_step = n_kv_step * GROUP
    nt_dims = (((1,), (1,)), ((), ()))

    @pl.when(own == 0)
    def _():
        for c in range(n_kv_step):
            cols = slice(c * HEAD_DIM, (c + 1) * HEAD_DIM)
            kbf_ref[c] = k_ref[:, cols].astype(BF16)
            for n in range(n_blocks):
                rows = slice(n * blk, (n + 1) * blk)
                kmean_ref[c, n:n + 1, :] = jnp.mean(k_ref[rows, cols], axis=0, keepdims=True)
                vt_ref[c, n] = jnp.transpose(v_ref[rows, cols]).astype(BF16)

    for g in range(n_heads_step):
        q = q_ref[:, g * HEAD_DIM:(g + 1) * HEAD_DIM]
        gate_t = lax.dot_general(kmean_ref[g // GROUP], q, nt_dims, precision=lax.Precision.HIGHEST,
                                 preferred_element_type=jnp.float32)
        n_iota = lax.broadcasted_iota(jnp.int32, gate_t.shape, 0)
        beaten_by = jnp.zeros(gate_t.shape, jnp.int32)
        for n in range(n_blocks):
            row = gate_t[n:n + 1, :]
            beats = jnp.logical_or(row > gate_t, jnp.logical_and(row == gate_t, n < n_iota))
            beaten_by = beaten_by + jnp.where(beats, 1, 0) * (n < own).astype(jnp.int32)
        sel_t = jnp.logical_and(beaten_by < MOBA_TOP_K, n_iota < own)
        sel_ref[g] = jnp.where(sel_t, 1.0, 0.0)
        qs_ref[g] = (q * (HEAD_DIM ** -0.5)).astype(BF16)
    m_ref[...] = jnp.full(m_ref.shape, -jnp.inf, jnp.float32)
    l_ref[...] = jnp.zeros(l_ref.shape, jnp.float32)
    acc_ref[...] = jnp.zeros(acc_ref.shape, jnp.float32)

    def score_block(j, keep_of_head):
        key_rows = pl.ds(pl.multiple_of(j * blk, blk), blk)
        for g in range(n_heads_step):
            s_t = lax.dot_general(kbf_ref[g // GROUP, key_rows, :], qs_ref[g], nt_dims,
                                  preferred_element_type=jnp.float32)
            s_t = jnp.where(keep_of_head(g), s_t, NEG_INF)
            s_ref[g, j] = s_t
            m_ref[g:g + 1, :] = jnp.maximum(m_ref[g:g + 1, :], jnp.max(s_t, axis=0, keepdims=True))

    def score_past(j, carry):
        score_block(j, lambda g: sel_ref[g, pl.ds(j, 1), :] > 0.5)
        return carry

    lax.fori_loop(0, own, score_past, 0)
    k_pos = lax.broadcasted_iota(jnp.int32, (blk, blk), 0)
    q_pos = lax.broadcasted_iota(jnp.int32, (blk, blk), 1)
    causal = k_pos <= q_pos
    score_block(own, lambda g: causal)

    def weigh_block(j, carry):
        for g in range(n_heads_step):
            p_t = jnp.exp(s_ref[g, j] - m_ref[g:g + 1, :])
            l_ref[g:g + 1, :] += jnp.sum(p_t, axis=0, keepdims=True)
            acc_ref[g] += jnp.dot(vt_ref[g // GROUP, j], p_t.astype(BF16),
                                  preferred_element_type=jnp.float32)
        return carry

    lax.fori_loop(0, own + 1, weigh_block, 0)

    for g in range(n_heads_step):
        out_t = acc_ref[g] / l_ref[g:g + 1, :]
        o_ref[:, g * HEAD_DIM:(g + 1) * HEAD_DIM] = jnp.transpose(out_t).astype(o_ref.dtype)


def moba_attention(q, k, v, attn_sample, *, n_batch, seq):
    rows, d = q.shape
    n_kv = k.shape[1] // HEAD_DIM
    n_blocks = seq // MOBA_BLOCK
    n_prompt_tiles = n_batch * n_blocks
    kvs = MOBA_KV_PER_STEP if n_kv % MOBA_KV_PER_STEP == 0 else 1
    hs = kvs * GROUP
    q_spec = pl.BlockSpec((MOBA_BLOCK, hs * HEAD_DIM), lambda h, t: (jnp.minimum(t, n_prompt_tiles - 1), h))
    kv_spec = pl.BlockSpec((seq, kvs * HEAD_DIM), lambda h, t: (jnp.minimum(t // n_blocks, n_batch - 1), h))
    return pl.pallas_call(
        functools.partial(_moba_kernel, n_blocks=n_blocks, n_prompt_tiles=n_prompt_tiles),
        grid=(n_kv // kvs, n_prompt_tiles + 1),
        in_specs=[q_spec, kv_spec, kv_spec, pl.BlockSpec((ROW_PAD, hs * HEAD_DIM), lambda h, t: (0, h))],
        out_specs=pl.BlockSpec((MOBA_BLOCK, hs * HEAD_DIM), lambda h, t: (t, h)),
        out_shape=jax.ShapeDtypeStruct((rows, d), BF16),
        scratch_shapes=[pltpu.VMEM((kvs, seq, HEAD_DIM), BF16),
                        pltpu.VMEM((kvs, n_blocks, HEAD_DIM, MOBA_BLOCK), BF16),
                        pltpu.VMEM((kvs, n_blocks, HEAD_DIM), jnp.float32),
                        pltpu.VMEM((hs, MOBA_BLOCK, HEAD_DIM), BF16),
                        pltpu.VMEM((hs, n_blocks, MOBA_BLOCK), jnp.float32),
                        pltpu.VMEM((hs, n_blocks, MOBA_BLOCK, MOBA_BLOCK), jnp.float32),
                        pltpu.VMEM((hs, MOBA_BLOCK), jnp.float32),
                        pltpu.VMEM((hs, MOBA_BLOCK), jnp.float32),
                        pltpu.VMEM((hs, HEAD_DIM, MOBA_BLOCK), jnp.float32)],
        compiler_params=_cparams(("parallel", "arbitrary")),
        name="moba_attention",
    )(q, k, v, attn_sample)


PAGES_PER_STEP = 16
GATHER_SLOTS = 8


def _page_mean_kernel(pt_ref, *refs, pages_per_block):
    del pt_ref
    page_refs, o_ref = refs[:-1], refs[-1]
    n_out = len(page_refs) // pages_per_block
    page_rows = page_refs[0].shape[0]
    for n in range(n_out):
        total = jnp.sum(page_refs[n * pages_per_block][...], axis=0)
        for r in range(1, pages_per_block):
            total = total + jnp.sum(page_refs[n * pages_per_block + r][...], axis=0)
        o_ref[n] = total / (pages_per_block * page_rows)


def paged_block_means(cache_k, page_table):
    _, page, n_kv, _ = cache_k.shape
    n_seq, n_pages = page_table.shape
    pages_per_block = MOBA_BLOCK // page
    blocks_per_step = PAGES_PER_STEP // pages_per_block
    n_steps = n_pages // PAGES_PER_STEP
    in_specs = [pl.BlockSpec((None, page, n_kv, HEAD_DIM),
                             lambda b, s, pt, r=r: (pt[b, s * PAGES_PER_STEP + r], 0, 0, 0))
                for r in range(PAGES_PER_STEP)]
    return pl.pallas_call(
        functools.partial(_page_mean_kernel, pages_per_block=pages_per_block),
        grid_spec=pltpu.PrefetchScalarGridSpec(
            num_scalar_prefetch=1,
            grid=(n_seq, n_steps),
            in_specs=in_specs,
            out_specs=pl.BlockSpec((None, blocks_per_step, n_kv, HEAD_DIM), lambda b, s, pt: (b, s, 0, 0))),
        out_shape=jax.ShapeDtypeStruct((n_seq, n_pages // pages_per_block, n_kv, HEAD_DIM), jnp.float32),
        compiler_params=_cparams(("parallel", "arbitrary")),
        name="page_means",
    )(page_table, *([cache_k] * PAGES_PER_STEP))


def _sample_topk_kernel(q_ref, kmean_ref, sel_ref):
    q = q_ref[...]
    n_heads = q.shape[0]
    n_kv, n_blocks, _ = kmean_ref.shape
    head_kv = lax.broadcasted_iota(jnp.int32, (n_heads, n_blocks), 0) // GROUP
    gate = jnp.zeros((n_heads, n_blocks), jnp.float32)
    for kv in range(n_kv):
        g = lax.dot_general(q, kmean_ref[kv], (((1,), (1,)), ((), ())),
                            precision=lax.Precision.HIGHEST, preferred_element_type=jnp.float32)
        gate = jnp.where(head_kv == kv, g, gate)
    blk = lax.broadcasted_iota(jnp.int32, gate.shape, 1).astype(jnp.float32)
    lane = lax.broadcasted_iota(jnp.int32, sel_ref.shape, 1)
    sel = jnp.zeros(sel_ref.shape, jnp.float32)
    for s in range(MOBA_TOP_K):
        best = jnp.max(gate, axis=-1, keepdims=True)
        idx = jnp.min(jnp.where(gate == best, blk, float(n_blocks)), axis=-1, keepdims=True)
        sel = jnp.where(lane == s, idx, sel)
        gate = jnp.where(blk == idx, -jnp.inf, gate)
    sel_ref[...] = sel.astype(jnp.int32)


def sample_topk(q_s, kmean_t):
    n_seq, n_heads, _ = q_s.shape
    _, n_kv, n_blocks, _ = kmean_t.shape
    return pl.pallas_call(
        _sample_topk_kernel,
        grid=(n_seq,),
        in_specs=[pl.BlockSpec((None, n_heads, HEAD_DIM), lambda b: (b, 0, 0)),
                  pl.BlockSpec((None, n_kv, n_blocks, HEAD_DIM), lambda b: (b, 0, 0, 0))],
        out_specs=pl.BlockSpec((None, n_heads, LANES), lambda b: (b, 0, 0)),
        out_shape=jax.ShapeDtypeStruct((n_seq, n_heads, LANES), jnp.int32),
        compiler_params=_cparams(("parallel",)),
        name="sample_topk",
    )(q_s, kmean_t)


def _sample_attend_kernel(pt_ref, sel_ref, q_ref, knew_ref, vnew_ref, ck_hbm, cv_hbm, o_ref,
                          kbuf, vbuf, sems, *, pages_per_block):
    n_seq, n_heads, _, _ = q_ref.shape
    n_pages_sel = kbuf.shape[1]
    n_items = n_seq * n_heads

    def page_copies(item, slot):
        b = item // n_heads
        kv = (item % n_heads) // GROUP
        copies = []
        for idx in range(n_pages_sel):
            blk = sel_ref[item * MOBA_TOP_K + idx // pages_per_block]
            pg = pt_ref[b, blk * pages_per_block + idx % pages_per_block]
            copies.append(pltpu.make_async_copy(ck_hbm.at[pg, :, kv, :], kbuf.at[slot, idx], sems.at[0, slot, idx]))
            copies.append(pltpu.make_async_copy(cv_hbm.at[pg, :, kv, :], vbuf.at[slot, idx], sems.at[1, slot, idx]))
        return copies

    n_slots = kbuf.shape[0]
    per_step = 4
    lookahead = n_slots - per_step
    assert n_items % per_step == 0 and lookahead % per_step == 0 and lookahead > 0
    for item in range(min(lookahead, n_items)):
        for c in page_copies(item, item % n_slots):
            c.start()

    def step(pair, carry):
        items = [pair * per_step + s for s in range(per_step)]
        for item in items:
            @pl.when(item + lookahead < n_items)
            def _(item=item):
                for c in page_copies(item + lookahead, (item + lookahead) % n_slots):
                    c.start()

        for item in items:
            for c in page_copies(item, item % n_slots):
                c.wait()
        for item in items:
            slot = item % n_slots
            b = item // n_heads
            h = item % n_heads
            n_keys = n_pages_sel * kbuf.shape[2]
            o_ref[b, h] = _sample_attend_one(q_ref[b, h], knew_ref[b, h // GROUP], vnew_ref[b, h // GROUP],
                                             kbuf[slot].reshape(n_keys, HEAD_DIM),
                                             vbuf[slot].reshape(n_keys, HEAD_DIM))
        return carry

    lax.fori_loop(0, n_items // per_step, step, 0)


def _sample_attend_one(q, k_new, v_new, k_sel, v_sel):
    q_scaled = q * (HEAD_DIM ** -0.5)
    qs = q_scaled.astype(BF16)
    qs8 = jnp.broadcast_to(q_scaled, (8, HEAD_DIM)).astype(BF16)
    s = lax.dot_general(qs8, k_sel.astype(BF16), (((1,), (1,)), ((), ())),
                        preferred_element_type=jnp.float32)
    knew = k_new.astype(BF16).astype(jnp.float32)
    s_own = jnp.sum(qs.astype(jnp.float32) * knew, axis=-1, keepdims=True)
    m = jnp.maximum(s_own, jnp.max(s[0:1, :], axis=-1, keepdims=True))
    p_own = jnp.exp(s_own - m)
    p = jnp.exp(s - m)
    inv = 1.0 / (p_own + jnp.sum(p[0:1, :], axis=-1, keepdims=True))
    vnew = v_new.astype(BF16).astype(jnp.float32)
    pv = jnp.dot((p * inv).astype(BF16), v_sel.astype(BF16), preferred_element_type=jnp.float32)
    return (p_own * inv).astype(BF16).astype(jnp.float32) * vnew + pv[0:1, :]


def sample_attend(q_s4, k_new4, v_new4, cache_k, cache_v, page_table, sel_flat):
    page = cache_k.shape[1]
    pages_per_block = MOBA_BLOCK // page
    n_pages_sel = MOBA_TOP_K * pages_per_block

    def whole(a):
        return pl.BlockSpec(a.shape, lambda i, pt, sel: (0,) * a.ndim)

    return pl.pallas_call(
        functools.partial(_sample_attend_kernel, pages_per_block=pages_per_block),
        grid_spec=pltpu.PrefetchScalarGridSpec(
            num_scalar_prefetch=2,
            grid=(1,),
            in_specs=[whole(q_s4), whole(k_new4), whole(v_new4),
                      pl.BlockSpec(memory_space=pl.ANY), pl.BlockSpec(memory_space=pl.ANY)],
            out_specs=whole(q_s4),
            scratch_shapes=[pltpu.VMEM((GATHER_SLOTS, n_pages_sel, page, HEAD_DIM), jnp.float32),
                            pltpu.VMEM((GATHER_SLOTS, n_pages_sel, page, HEAD_DIM), jnp.float32),
                            pltpu.SemaphoreType.DMA((2, GATHER_SLOTS, n_pages_sel))]),
        out_shape=jax.ShapeDtypeStruct(q_s4.shape, jnp.float32),
        compiler_params=_cparams(("arbitrary",)),
        name="sample_attend",
    )(page_table, sel_flat, q_s4, k_new4, v_new4, cache_k, cache_v)


def _rope_tables(positions):
    inv_freq = ROPE_THETA ** (-jnp.arange(ROT_HALF, dtype=jnp.float32) * 2.0 / ROT_DIM)
    ang = positions.astype(jnp.float32)[:, None] * inv_freq[None, :]
    cos, sin = jnp.cos(ang), jnp.sin(ang)
    rest = HEAD_DIM - ROT_DIM
    ones = jnp.ones((positions.shape[0], rest), jnp.float32)
    zeros = jnp.zeros((positions.shape[0], rest), jnp.float32)
    return (jnp.concatenate([cos, cos, ones], axis=1), jnp.concatenate([-sin, sin, zeros], axis=1))


def kernel(x_prompt, x_sample, state_conv, cache_k, cache_v, page_table, a_norm, a_w_in, a_conv_w, a_w_out,
           kv_norm, w_k, w_v, b_norm, w_q, w_o, ffn_norm, w_gate, w_up, w_down, final_norm):
    n_batch, seq, d_model = x_prompt.shape
    n_sample = x_sample.shape[0]
    assert x_sample.shape[1] == 1 and n_sample <= ROW_PAD
    assert a_w_in.shape[0] == 1 and w_q.shape[0] == 1, "one short-conv layer then one MoBA layer"
    _, page, n_kv, _ = cache_k.shape
    n_heads = d_model // HEAD_DIM
    past_len = page_table.shape[1] * page
    assert past_len % MOBA_BLOCK == 0 and past_len // MOBA_BLOCK >= MOBA_TOP_K
    rows_p = n_batch * seq
    rows = rows_p + ROW_PAD
    sample_blk = rows_p // ROW_PAD

    tm_down, tm_big = (_row_tile(rows, t) for t in (832, 1664))
    tr_p, tm_p = _row_tile(rows_p, 256), _row_tile(rows_p, 1024)

    x_sample_pad = jnp.concatenate([x_sample.reshape(n_sample, d_model),
                                    jnp.zeros((ROW_PAD - n_sample, d_model), x_sample.dtype)], axis=0)
    pos_p = jnp.tile(jnp.arange(seq, dtype=jnp.int32), n_batch)
    pos_s = jnp.concatenate([jnp.full((n_sample,), past_len, jnp.int32),
                             jnp.zeros((ROW_PAD - n_sample,), jnp.int32)])
    rope_p, rope_s = _rope_tables(pos_p), _rope_tables(pos_s)
    rope_all = tuple(jnp.concatenate([p, s], axis=0) for p, s in zip(rope_p, rope_s))

    def ffn(h, h_scaled, h_ssq, layer, next_gains):
        act, w_down_bf16 = matmul(h_scaled, [w_gate, w_up], layer=layer, tm=tm_big, tn=256, x_buffers=1,
                                  mode="swiglu", out_dtype=BF16, x_ssq=h_ssq, side_cast=(w_down, layer),
                                  name="ffn_gate_up")
        return matmul(act, [w_down_bf16], tm=tm_down, tn=256, mode="residual", extra=(h,),
                      next_gains=next_gains, name="ffn_down")

    xn, x = rmsnorm_in(x_prompt.reshape(rows_p, d_model), x_sample_pad, a_norm, rows_per_step=tr_p)
    g, conv_prompt, conv_sample = conv_in(xn, a_w_in, a_conv_w[0], state_conv[0], n_batch=n_batch, seq=seq,
                                          n_sample=n_sample, tm=_row_tile(rows, 1040), tc=256)
    h, h_scaled, h_ssq = matmul(g, [a_w_out], layer=0, tm=tm_big, tn=512, x_buffers=1, mode="residual",
                                extra=(x,), next_gains=ffn_norm[0:1], name="conv_out")
    h, kv_in, q_in, h_ssq = ffn(h, h_scaled, h_ssq, 0, jnp.stack([kv_norm, b_norm[0]]))

    k_p, v_p = matmul(kv_in, [w_k, w_v], tm=tm_p, tn=256, mode="rotary_plain", extra=rope_p, x_ssq=h_ssq,
                      name="kv_proj", n_row_blocks=rows_p // tm_p)
    k_s, v_s = matmul(kv_in, [w_k, w_v], tm=ROW_PAD, tn=256, mode="rotary_plain", extra=rope_s, x_ssq=h_ssq,
                      name="kv_proj_sample", row_block0=sample_blk, n_row_blocks=1)
    q = matmul(q_in, [w_q], layer=0, tm=tm_big, tn=512, x_buffers=1, mode="rotary", extra=rope_all,
               x_ssq=h_ssq, name="q_proj")

    kmean = paged_block_means(cache_k, page_table)
    q_s = q[rows_p:rows_p + n_sample].reshape(n_sample, n_heads, HEAD_DIM)
    sel = sample_topk(q_s, kmean.transpose(0, 2, 1, 3))[:, :, :MOBA_TOP_K].reshape(-1)
    k_s = k_s[:n_sample].reshape(n_sample, n_kv, 1, HEAD_DIM)
    v_s = v_s[:n_sample].reshape(n_sample, n_kv, 1, HEAD_DIM)
    attn_s = sample_attend(q_s.reshape(n_sample, n_heads, 1, HEAD_DIM), k_s, v_s, cache_k, cache_v,
                           page_table, sel)
    attn_s = jnp.concatenate([attn_s.reshape(n_sample, d_model),
                              jnp.zeros((ROW_PAD - n_sample, d_model), jnp.float32)], axis=0)
    attn = moba_attention(q, k_p, v_p, attn_s, n_batch=n_batch, seq=seq)

    h, h_scaled, h_ssq = matmul(attn, [w_o], layer=0, tm=tm_big, tn=512, x_buffers=1, mode="residual",
                                extra=(h,), next_gains=ffn_norm[1:2], name="attn_out")
    h = ffn(h, h_scaled, h_ssq, 1, None)
    y_p, = rmsnorm(h, final_norm[None, :], [jnp.float32], rows_per_step=tr_p, n_row_blocks=rows_p // tr_p)
    y_s, = rmsnorm(h, final_norm[None, :], [jnp.float32], rows_per_step=ROW_PAD, row_block0=sample_blk,
                   n_row_blocks=1)

    return (y_p.reshape(n_batch, seq, d_model), y_s[:n_sample].reshape(n_sample, 1, d_model),
            conv_prompt[None], conv_sample[None],
            k_p.reshape(n_batch, seq, n_kv, HEAD_DIM), v_p.reshape(n_batch, seq, n_kv, HEAD_DIM),
            k_s.reshape(n_sample, 1, n_kv, HEAD_DIM), v_s.reshape(n_sample, 1, n_kv, HEAD_DIM))
```

```python
import functools

import jax
import jax.numpy as jnp
from jax import lax
from jax.experimental import pallas as pl
from jax.experimental.pallas import tpu as pltpu

HEAD_DIM = 128
GROUP = 4
ROT_DIM = HEAD_DIM // 4
ROT_HALF = ROT_DIM // 2
ROPE_THETA = 500000.0
MOBA_BLOCK = 256
MOBA_TOP_K = 3
CONV_WIDTH = 3
RMS_EPS = 1e-6
NEG_INF = -1e30

LANES = 128
BF16_SUBLANES = 16
VMEM_LIMIT_BYTES = 58 * 1024 * 1024

ROW_PAD = 128
BF16 = jnp.bfloat16


def _row_tile(rows, target):
    return max(t for t in range(BF16_SUBLANES, target + 1, BF16_SUBLANES) if rows % t == 0)


def _cparams(semantics):
    return pltpu.CompilerParams(dimension_semantics=semantics, vmem_limit_bytes=VMEM_LIMIT_BYTES)


def _rmsnorm_kernel(x_ref, g_ref, *o_refs):
    x = x_ref[...]
    xn = x * lax.rsqrt(jnp.mean(x * x, axis=-1, keepdims=True) + RMS_EPS)
    for i, o_ref in enumerate(o_refs):
        o_ref[...] = (xn * g_ref[i:i + 1, :]).astype(o_ref.dtype)


def rmsnorm(x, gains, out_dtypes, *, rows_per_step, row_block0=0, n_row_blocks=None):
    rows, d = x.shape
    n_out = gains.shape[0]
    if n_row_blocks is None:
        n_row_blocks = rows // rows_per_step
    out_spec = pl.BlockSpec((rows_per_step, d), lambda r: (r, 0))
    return pl.pallas_call(
        _rmsnorm_kernel,
        grid=(n_row_blocks,),
        in_specs=[pl.BlockSpec((rows_per_step, d), lambda r: (r + row_block0, 0)),
                  pl.BlockSpec((n_out, d), lambda r: (0, 0))],
        out_specs=[out_spec] * n_out,
        out_shape=[jax.ShapeDtypeStruct((n_row_blocks * rows_per_step, d), dt) for dt in out_dtypes],
        compiler_params=_cparams(("parallel",)),
        name="rmsnorm",
    )(x, gains)


def _rmsnorm_in_kernel(xp_ref, xs_ref, g_ref, xn_ref, xc_ref, *, n_prompt_steps):
    step = pl.program_id(0)

    def emit(x, rows):
        xn = x * lax.rsqrt(jnp.mean(x * x, axis=-1, keepdims=True) + RMS_EPS)
        xn_ref[rows, :] = (xn * g_ref[...]).astype(xn_ref.dtype)
        xc_ref[rows, :] = x

    @pl.when(step < n_prompt_steps)
    def _():
        emit(xp_ref[...], slice(None))

    @pl.when(step == n_prompt_steps)
    def _():
        emit(xs_ref[...], slice(0, ROW_PAD))


def rmsnorm_in(x_prompt2d, x_sample_pad, gain, *, rows_per_step):
    rows_p, d = x_prompt2d.shape
    n_prompt_steps = rows_p // rows_per_step
    assert rows_p % rows_per_step == 0 and ROW_PAD <= rows_per_step
    rows = rows_p + ROW_PAD
    out_spec = pl.BlockSpec((rows_per_step, d), lambda r: (r, 0))
    return pl.pallas_call(
        functools.partial(_rmsnorm_in_kernel, n_prompt_steps=n_prompt_steps),
        grid=(n_prompt_steps + 1,),
        in_specs=[pl.BlockSpec((rows_per_step, d), lambda r: (jnp.minimum(r, n_prompt_steps - 1), 0)),
                  pl.BlockSpec((ROW_PAD, d), lambda r: (0, 0)),
                  pl.BlockSpec((1, d), lambda r: (0, 0))],
        out_specs=[out_spec, out_spec],
        out_shape=[jax.ShapeDtypeStruct((rows, d), BF16), jax.ShapeDtypeStruct((rows, d), jnp.float32)],
        compiler_params=_cparams(("arbitrary",)),
        name="rmsnorm_in",
    )(x_prompt2d, x_sample_pad, gain)


def _rotary_epilogue(y, cos, sin_signed):
    lane = lax.broadcasted_iota(jnp.int32, (y.shape[0], HEAD_DIM), 1)
    first_half = lane < ROT_HALF
    outs = []
    for h in range(y.shape[1] // HEAD_DIM):
        yh = y[:, h * HEAD_DIM:(h + 1) * HEAD_DIM]
        from_above = pltpu.roll(yh, HEAD_DIM - ROT_HALF, axis=1)
        from_below = pltpu.roll(yh, ROT_HALF, axis=1)
        partner = jnp.where(first_half, from_above, from_below)
        outs.append(yh * cos + partner * sin_signed)
    return outs[0] if len(outs) == 1 else jnp.concatenate(outs, axis=1)


def _row_pieces(tm):
    split = tm // 2 // BF16_SUBLANES * BF16_SUBLANES
    return [slice(0, tm)] if split == 0 else [slice(0, split), slice(split, tm)]


def _matmul_kernel(*refs, n_w, mode, normed_x, n_gains, side_cast, split_rows):
    refs = list(refs)
    x_ref = refs.pop(0)
    w_refs = [refs.pop(0) for _ in range(n_w)]
    ssq_in_ref = refs.pop(0) if normed_x else None
    res_ref = refs.pop(0) if mode == "residual" else None
    gains_ref = refs.pop(0) if n_gains else None
    rope_refs = [refs.pop(0) for _ in range(2)] if mode in ("rotary", "rotary_plain") else None
    side_in_ref = refs.pop(0) if side_cast else None
    o_refs = [refs.pop(0) for _ in range(2 if mode == "rotary_plain" else 1)]
    scaled_refs = [refs.pop(0) for _ in range(n_gains)]
    ssq_out_ref = refs.pop(0) if n_gains else None
    side_out_ref = refs.pop(0) if side_cast else None
    assert not refs
    i, j = pl.program_id(0), pl.program_id(1)

    tm, k_dim = x_ref.shape
    weights = [w_ref[...].astype(x_ref.dtype) for w_ref in w_refs]
    ssq_parts = []
    for rows in (_row_pieces(tm) if split_rows else [slice(0, tm)]):
        x = x_ref[rows, :]
        vals = [jnp.dot(x, w, preferred_element_type=jnp.float32) for w in weights]
        if normed_x:
            inv_rms = lax.rsqrt(ssq_in_ref[rows, :] * (1.0 / k_dim) + RMS_EPS)
            inv_rms = jnp.concatenate([inv_rms] * (vals[0].shape[1] // LANES), axis=1)
            vals = [v * inv_rms for v in vals]

        if mode == "swiglu":
            g, u = vals
            ys = [(g * jax.nn.sigmoid(g)) * u]
        elif mode == "residual":
            ys = [res_ref[rows, :] + vals[0]]
        elif mode == "rotary":
            ys = [_rotary_epilogue(vals[0], rope_refs[0][rows, :], rope_refs[1][rows, :])]
        elif mode == "rotary_plain":
            ys = [_rotary_epilogue(vals[0], rope_refs[0][rows, :], rope_refs[1][rows, :]), vals[1]]
        else:
            ys = vals
        for o_ref, y in zip(o_refs, ys):
            o_ref[rows, :] = y.astype(o_ref.dtype)

        if n_gains:
            h = ys[0]
            for n, scaled_ref in enumerate(scaled_refs):
                scaled_ref[rows, :] = (h * gains_ref[n:n + 1, :]).astype(scaled_ref.dtype)
            ssq_parts.append((rows, jnp.broadcast_to(jnp.sum(h * h, axis=1, keepdims=True),
                                                     (h.shape[0], LANES))))

    if n_gains:
        @pl.when(j == 0)
        def _():
            for rows, part in ssq_parts:
                ssq_out_ref[rows, :] = part

        @pl.when(j > 0)
        def _():
            for rows, part in ssq_parts:
                ssq_out_ref[rows, :] += part

    if side_cast:
        @pl.when(i == 0)
        def _():
            side_out_ref[...] = side_in_ref[...].astype(side_out_ref.dtype)


def matmul(x, ws, *, tm, tn, layer=None, mode="plain", extra=(), out_dtype=jnp.float32, name="matmul",
           row_block0=0, n_row_blocks=None, x_buffers=2, x_ssq=None, next_gains=None, side_cast=None,
           split_rows=True):
    m, k_dim = x.shape
    n = ws[0].shape[-1]
    tn = min(tn, n)
    n_j = n // tn
    if n_row_blocks is None:
        n_row_blocks = m // tm
    rows_out = n_row_blocks * tm
    assert n % tn == 0 and (row_block0 + n_row_blocks) * tm <= m
    assert next_gains is None or mode == "residual"
    n_gains = 0 if next_gains is None else next_gains.shape[0]

    out_tile = pl.BlockSpec((tm, tn), lambda i, j: (i, j))

    def per_row(width):
        return pl.BlockSpec((tm, width), lambda i, j: (i, 0))

    x_mode = {} if x_buffers == 2 else {"pipeline_mode": pl.Buffered(x_buffers)}
    operands = [x]
    in_specs = [pl.BlockSpec((tm, k_dim), lambda i, j: (i + row_block0, 0), **x_mode)]
    operands += ws
    if layer is None:
        in_specs += [pl.BlockSpec((k_dim, tn), lambda i, j: (0, j)) for _ in ws]
    else:
        in_specs += [pl.BlockSpec((None, k_dim, tn), lambda i, j: (layer, 0, j)) for _ in ws]
    if x_ssq is not None:
        operands.append(x_ssq)
        in_specs.append(pl.BlockSpec((tm, LANES), lambda i, j: (i + row_block0, 0)))
    if mode == "residual":
        operands.append(extra[0])
        in_specs.append(out_tile)
    if n_gains:
        operands.append(next_gains)
        in_specs.append(pl.BlockSpec((n_gains, tn), lambda i, j: (0, j)))
    if mode in ("rotary", "rotary_plain"):
        operands += extra
        in_specs += [per_row(HEAD_DIM)] * 2

    n_main = 2 if mode == "rotary_plain" else 1
    out_specs = [out_tile] * n_main
    out_shapes = [jax.ShapeDtypeStruct((rows_out, n), out_dtype)] * n_main
    if n_gains:
        out_specs += [out_tile] * n_gains + [per_row(LANES)]
        out_shapes += [jax.ShapeDtypeStruct((rows_out, n), BF16)] * n_gains
        out_shapes.append(jax.ShapeDtypeStruct((rows_out, LANES), jnp.float32))
    if side_cast is not None:
        side_w, side_layer = side_cast
        _, side_rows, side_cols = side_w.shape
        slab = side_rows // n_j
        assert side_rows % n_j == 0 and slab % BF16_SUBLANES == 0

        def slab_index(i, j):
            return jnp.where(i == 0, j, n_j - 1)

        operands.append(side_w)
        in_specs.append(pl.BlockSpec((None, slab, side_cols), lambda i, j: (side_layer, slab_index(i, j), 0)))
        out_specs.append(pl.BlockSpec((slab, side_cols), lambda i, j: (slab_index(i, j), 0)))
        out_shapes.append(jax.ShapeDtypeStruct((side_rows, side_cols), BF16))

    outs = pl.pallas_call(
        functools.partial(_matmul_kernel, n_w=len(ws), mode=mode, normed_x=x_ssq is not None,
                          n_gains=n_gains, side_cast=side_cast is not None, split_rows=split_rows),
        grid=(n_row_blocks, n_j),
        in_specs=in_specs,
        out_specs=out_specs,
        out_shape=out_shapes,
        compiler_params=_cparams(("arbitrary" if side_cast is not None else "parallel",
                                  "arbitrary" if (n_gains or side_cast is not None) else "parallel")),
        name=name,
    )(*operands)
    return outs if len(outs) > 1 else outs[0]


def _conv_in_kernel(x_ref, wb_ref, wc_ref, wx_ref, st0_ref, st1_ref, cw_ref, g_ref, state_ref, us_ref,
                    carry_ref, *, seq, n_batch, split):
    i = pl.program_id(1)
    tm = x_ref.shape[0]
    halo = CONV_WIDTH - 1

    @pl.when(i == 0)
    def _():
        carry_ref[...] = jnp.zeros(carry_ref.shape, carry_ref.dtype)

    start = (i * tm + seq - 1) // seq * seq - i * tm
    start = jnp.where(start + i * tm < n_batch * seq, start, tm)
    weights = [w_ref[...].astype(x_ref.dtype) for w_ref in (wb_ref, wc_ref, wx_ref)]
    before2, before1 = carry_ref[0:1, :], carry_ref[1:2, :]
    pieces = []
    for lo, hi in ((0, split), (split, tm)):
        x = x_ref[lo:hi, :]
        b, c, x_in = [jnp.dot(x, w, preferred_element_type=jnp.float32) for w in weights]
        u = c * x_in
        t = lax.broadcasted_iota(jnp.int32, u.shape, 0) + lo
        u_prev1 = jnp.where(t == lo, before1, pltpu.roll(u, 1, axis=0))
        u_prev2 = jnp.where(t == lo, before2, jnp.where(t == lo + 1, before1, pltpu.roll(u, 2, axis=0)))
        u_prev1 = jnp.where(t == start, 0.0, u_prev1)
        u_prev2 = jnp.where(t == start, 0.0, jnp.where(t == start + 1, 0.0, u_prev2))
        conv = cw_ref[0:1, :] * u_prev2 + cw_ref[1:2, :] * u_prev1 + cw_ref[2:3, :] * u
        g_ref[lo:hi, :] = (b * conv).astype(g_ref.dtype)
        before2, before1 = u[hi - lo - 2:hi - lo - 1, :], u[hi - lo - 1:, :]
        pieces.append((lo, hi, b, u))
    carry_ref[0:1, :] = before2
    carry_ref[1:2, :] = before1

    def rows_of(values, first, count):
        lo, _, b, u = next(p for p in pieces if p[0] <= first and first + count <= p[1])
        return {"b": b, "u": u}[values][first - lo:first - lo + count, :]

    for n in range(n_batch):
        tile, row = divmod((n + 1) * seq - halo, tm)

        @pl.when(i == tile)
        def _(n=n, row=row):
            state_ref[n] = rows_of("u", row, halo)

    @pl.when(i == pl.num_programs(1) - 1)
    def _():
        us = rows_of("u", tm - ROW_PAD, ROW_PAD)
        conv_s = cw_ref[0:1, :] * st0_ref[...] + cw_ref[1:2, :] * st1_ref[...] + cw_ref[2:3, :] * us
        g_ref[tm - ROW_PAD:, :] = (rows_of("b", tm - ROW_PAD, ROW_PAD) * conv_s).astype(g_ref.dtype)
        us_ref[...] = us


def conv_in(xn, w_in, conv_w, state_sample, *, n_batch, seq, n_sample, tm, tc):
    rows, d = xn.shape
    n_c = d // tc
    halo = CONV_WIDTH - 1
    split = tm // 2 // BF16_SUBLANES * BF16_SUBLANES
    assert rows % tm == 0 and tm <= seq and rows == n_batch * seq + ROW_PAD and split <= tm - ROW_PAD
    for n in range(1, n_batch + 1):
        state_row = (n * seq - halo) % tm
        assert (n * seq) % tm != tm - 1 and state_row + halo <= tm
        assert not state_row < split < state_row + halo
    pad = jnp.zeros((ROW_PAD - n_sample, d), jnp.float32)
    st0 = jnp.concatenate([state_sample[:, 0, :], pad], axis=0)
    st1 = jnp.concatenate([state_sample[:, 1, :], pad], axis=0)

    def w_spec(gate):
        return pl.BlockSpec((None, d, tc), lambda j, i: (0, 0, gate * n_c + j))

    sample_spec = pl.BlockSpec((ROW_PAD, tc), lambda j, i: (0, j))
    g, st_prompt, u_sample = pl.pallas_call(
        functools.partial(_conv_in_kernel, seq=seq, n_batch=n_batch, split=split),
        grid=(n_c, rows // tm),
        in_specs=[pl.BlockSpec((tm, d), lambda j, i: (i, 0)), w_spec(0), w_spec(1), w_spec(2),
                  sample_spec, sample_spec, pl.BlockSpec((CONV_WIDTH, tc), lambda j, i: (0, j))],
        out_specs=[pl.BlockSpec((tm, tc), lambda j, i: (i, j)),
                   pl.BlockSpec((n_batch, halo, tc), lambda j, i: (0, 0, j)),
                   sample_spec],
        out_shape=[jax.ShapeDtypeStruct((rows, d), BF16),
                   jax.ShapeDtypeStruct((n_batch, halo, d), jnp.float32),
                   jax.ShapeDtypeStruct((ROW_PAD, d), jnp.float32)],
        scratch_shapes=[pltpu.VMEM((8, tc), jnp.float32)],
        compiler_params=_cparams(("parallel", "arbitrary")),
        name="conv_in",
    )(xn, w_in, w_in, w_in, st0, st1, conv_w)
    st_sample = jnp.stack([state_sample[:, 1, :], u_sample[:n_sample]], axis=1)
    return g, st_prompt, st_sample


MOBA_KV_PER_STEP = 2


def _moba_kernel(q_ref, k_ref, v_ref, sample_ref, o_ref, *scratch, n_blocks, n_prompt_tiles):
    tile = pl.program_id(1)

    @pl.when(tile < n_prompt_tiles)
    def _():
        _moba_prompt_tile(q_ref, k_ref, v_ref, o_ref, *scratch, own=tile % n_blocks, n_blocks=n_blocks)

    @pl.when(tile == n_prompt_tiles)
    def _():
        o_ref[0:ROW_PAD, :] = sample_ref[...].astype(o_ref.dtype)


def _moba_prompt_tile(q_ref, k_ref, v_ref, o_ref, kbf_ref, vt_ref, kmean_ref, qs_ref, sel_ref,
                      s_ref, m_ref, l_ref, acc_ref, *, own, n_blocks):
    blk = MOBA_BLOCK
    n_kv_step = kbf_ref.shape[0]
    n_heads_step = n_kv_step * GROUP
    nt_dims = (((1,), (1,)), ((), ()))

    @pl.when(own == 0)
    def _():
        for c in range(n_kv_step):
            cols = slice(c * HEAD_DIM, (c + 1) * HEAD_DIM)
            kbf_ref[c] = k_ref[:, cols].astype(BF16)
            for n in range(n_blocks):
                rows = slice(n * blk, (n + 1) * blk)
                kmean_ref[c, n:n + 1, :] = jnp.mean(k_ref[rows, cols], axis=0, keepdims=True)
                vt_ref[c, n] = jnp.transpose(v_ref[rows, cols]).astype(BF16)

    for g in range(n_heads_step):
        q = q_ref[:, g * HEAD_DIM:(g + 1) * HEAD_DIM]
        gate_t = lax.dot_general(kmean_ref[g // GROUP], q, nt_dims, precision=lax.Precision.HIGHEST,
                                 preferred_element_type=jnp.float32)
        n_iota = lax.broadcasted_iota(jnp.int32, gate_t.shape, 0)
        beaten_by = jnp.zeros(gate_t.shape, jnp.int32)
        for n in range(n_blocks):
            row = gate_t[n:n + 1, :]
            beats = jnp.logical_or(row > gate_t, jnp.logical_and(row == gate_t, n < n_iota))
            beaten_by = beaten_by + jnp.where(beats, 1, 0) * (n < own).astype(jnp.int32)
        sel_t = jnp.logical_and(beaten_by < MOBA_TOP_K, n_iota < own)
        sel_ref[g] = jnp.where(sel_t, 1.0, 0.0)
        qs_ref[g] = (q * (HEAD_DIM ** -0.5)).astype(BF16)
    m_ref[...] = jnp.full(m_ref.shape, -jnp.inf, jnp.float32)
    l_ref[...] = jnp.zeros(l_ref.shape, jnp.float32)
    acc_ref[...] = jnp.zeros(acc_ref.shape, jnp.float32)

    def in_pairs(n, do_blocks):
        def pair(p, carry):
            do_blocks([2 * p, 2 * p + 1])
            return carry

        lax.fori_loop(0, n // 2, pair, 0)

        @pl.when(n % 2 == 1)
        def _():
            do_blocks([n - 1])

    def score_blocks(js, keep_of_head):
        for g in range(n_heads_step):
            m = m_ref[g:g + 1, :]
            for j in js:
                key_rows = pl.ds(pl.multiple_of(j * blk, blk), blk)
                s_t = lax.dot_general(kbf_ref[g // GROUP, key_rows, :], qs_ref[g], nt_dims,
                                      preferred_element_type=jnp.float32)
                s_t = jnp.where(keep_of_head(g, j), s_t, NEG_INF)
                s_ref[g, j] = s_t
                m = jnp.maximum(m, jnp.max(s_t, axis=0, keepdims=True))
            m_ref[g:g + 1, :] = m

    in_pairs(own, lambda js: score_blocks(js, lambda g, j: sel_ref[g, pl.ds(j, 1), :] > 0.5))
    k_pos = lax.broadcasted_iota(jnp.int32, (blk, blk), 0)
    q_pos = lax.broadcasted_iota(jnp.int32, (blk, blk), 1)
    causal = k_pos <= q_pos
    score_blocks([own], lambda g, j: causal)

    def weigh_blocks(js):
        for g in range(n_heads_step):
            l_new, acc_new = l_ref[g:g + 1, :], acc_ref[g]
            for j in js:
                p_t = jnp.exp(s_ref[g, j] - m_ref[g:g + 1, :])
                l_new = l_new + jnp.sum(p_t, axis=0, keepdims=True)
                acc_new = acc_new + jnp.dot(vt_ref[g // GROUP, j], p_t.astype(BF16),
                                            preferred_element_type=jnp.float32)
            l_ref[g:g + 1, :] = l_new
            acc_ref[g] = acc_new

    in_pairs(own + 1, weigh_blocks)

    for g in range(n_heads_step):
        out_t = acc_ref[g] / l_ref[g:g + 1, :]
        o_ref[:, g * HEAD_DIM:(g + 1) * HEAD_DIM] = jnp.transpose(out_t).astype(o_ref.dtype)


def moba_attention(q, k, v, attn_sample, *, n_batch, seq):
    rows, d = q.shape
    n_kv = k.shape[1] // HEAD_DIM
    n_blocks = seq // MOBA_BLOCK
    n_prompt_tiles = n_batch * n_blocks
    kvs = MOBA_KV_PER_STEP if n_kv % MOBA_KV_PER_STEP == 0 else 1
    hs = kvs * GROUP
    q_spec = pl.BlockSpec((MOBA_BLOCK, hs * HEAD_DIM), lambda h, t: (jnp.minimum(t, n_prompt_tiles - 1), h))
    kv_spec = pl.BlockSpec((seq, kvs * HEAD_DIM), lambda h, t: (jnp.minimum(t // n_blocks, n_batch - 1), h))
    return pl.pallas_call(
        functools.partial(_moba_kernel, n_blocks=n_blocks, n_prompt_tiles=n_prompt_tiles),
        grid=(n_kv // kvs, n_prompt_tiles + 1),
        in_specs=[q_spec, kv_spec, kv_spec, pl.BlockSpec((ROW_PAD, hs * HEAD_DIM), lambda h, t: (0, h))],
        out_specs=pl.BlockSpec((MOBA_BLOCK, hs * HEAD_DIM), lambda h, t: (t, h)),
        out_shape=jax.ShapeDtypeStruct((rows, d), BF16),
        scratch_shapes=[pltpu.VMEM((kvs, seq, HEAD_DIM), BF16),
                        pltpu.VMEM((kvs, n_blocks, HEAD_DIM, MOBA_BLOCK), BF16),
                        pltpu.VMEM((kvs, n_blocks, HEAD_DIM), jnp.float32),
                        pltpu.VMEM((hs, MOBA_BLOCK, HEAD_DIM), BF16),
                        pltpu.VMEM((hs, n_blocks, MOBA_BLOCK), jnp.float32),
                        pltpu.VMEM((hs, n_blocks, MOBA_BLOCK, MOBA_BLOCK), jnp.float32),
                        pltpu.VMEM((hs, MOBA_BLOCK), jnp.float32),
                        pltpu.VMEM((hs, MOBA_BLOCK), jnp.float32),
                        pltpu.VMEM((hs, HEAD_DIM, MOBA_BLOCK), jnp.float32)],
        compiler_params=_cparams(("parallel", "arbitrary")),
        name="moba_attention",
    )(q, k, v, attn_sample)


PAGES_PER_STEP = 16
GATHER_SLOTS = 8


def _page_mean_kernel(pt_ref, *refs, pages_per_block):
    del pt_ref
    page_refs, o_ref = refs[:-1], refs[-1]
    n_out = len(page_refs) // pages_per_block
    page_rows = page_refs[0].shape[0]
    for n in range(n_out):
        total = jnp.sum(page_refs[n * pages_per_block][...], axis=0)
        for r in range(1, pages_per_block):
            total = total + jnp.sum(page_refs[n * pages_per_block + r][...], axis=0)
        o_ref[n] = total / (pages_per_block * page_rows)


def paged_block_means(cache_k, page_table):
    _, page, n_kv, _ = cache_k.shape
    n_seq, n_pages = page_table.shape
    pages_per_block = MOBA_BLOCK // page
    blocks_per_step = PAGES_PER_STEP // pages_per_block
    n_steps = n_pages // PAGES_PER_STEP
    in_specs = [pl.BlockSpec((None, page, n_kv, HEAD_DIM),
                             lambda b, s, pt, r=r: (pt[b, s * PAGES_PER_STEP + r], 0, 0, 0))
                for r in range(PAGES_PER_STEP)]
    return pl.pallas_call(
        functools.partial(_page_mean_kernel, pages_per_block=pages_per_block),
        grid_spec=pltpu.PrefetchScalarGridSpec(
            num_scalar_prefetch=1,
            grid=(n_seq, n_steps),
            in_specs=in_specs,
            out_specs=pl.BlockSpec((None, blocks_per_step, n_kv, HEAD_DIM), lambda b, s, pt: (b, s, 0, 0))),
        out_shape=jax.ShapeDtypeStruct((n_seq, n_pages // pages_per_block, n_kv, HEAD_DIM), jnp.float32),
        compiler_params=_cparams(("parallel", "arbitrary")),
        name="page_means",
    )(page_table, *([cache_k] * PAGES_PER_STEP))


def _sample_topk_kernel(q_ref, kmean_ref, sel_ref):
    q = q_ref[...]
    n_heads = q.shape[0]
    n_kv, n_blocks, _ = kmean_ref.shape
    head_kv = lax.broadcasted_iota(jnp.int32, (n_heads, n_blocks), 0) // GROUP
    gate = jnp.zeros((n_heads, n_blocks), jnp.float32)
    for kv in range(n_kv):
        g = lax.dot_general(q, kmean_ref[kv], (((1,), (1,)), ((), ())),
                            precision=lax.Precision.HIGHEST, preferred_element_type=jnp.float32)
        gate = jnp.where(head_kv == kv, g, gate)
    blk = lax.broadcasted_iota(jnp.int32, gate.shape, 1).astype(jnp.float32)
    lane = lax.broadcasted_iota(jnp.int32, sel_ref.shape, 1)
    sel = jnp.zeros(sel_ref.shape, jnp.float32)
    for s in range(MOBA_TOP_K):
        best = jnp.max(gate, axis=-1, keepdims=True)
        idx = jnp.min(jnp.where(gate == best, blk, float(n_blocks)), axis=-1, keepdims=True)
        sel = jnp.where(lane == s, idx, sel)
        gate = jnp.where(blk == idx, -jnp.inf, gate)
    sel_ref[...] = sel.astype(jnp.int32)


def sample_topk(q_s, kmean_t):
    n_seq, n_heads, _ = q_s.shape
    _, n_kv, n_blocks, _ = kmean_t.shape
    return pl.pallas_call(
        _sample_topk_kernel,
        grid=(n_seq,),
        in_specs=[pl.BlockSpec((None, n_heads, HEAD_DIM), lambda b: (b, 0, 0)),
                  pl.BlockSpec((None, n_kv, n_blocks, HEAD_DIM), lambda b: (b, 0, 0, 0))],
        out_specs=pl.BlockSpec((None, n_heads, LANES), lambda b: (b, 0, 0)),
        out_shape=jax.ShapeDtypeStruct((n_seq, n_heads, LANES), jnp.int32),
        compiler_params=_cparams(("parallel",)),
        name="sample_topk",
    )(q_s, kmean_t)


def _sample_attend_kernel(pt_ref, sel_ref, q_ref, knew_ref, vnew_ref, ck_hbm, cv_hbm, o_ref,
                          kbuf, vbuf, sems, *, pages_per_block):
    n_seq, n_heads, _, _ = q_ref.shape
    n_pages_sel = kbuf.shape[1]
    n_items = n_seq * n_heads

    def page_copies(item, slot):
        b = item // n_heads
        kv = (item % n_heads) // GROUP
        copies = []
        for idx in range(n_pages_sel):
            blk = sel_ref[item * MOBA_TOP_K + idx // pages_per_block]
            pg = pt_ref[b, blk * pages_per_block + idx % pages_per_block]
            copies.append(pltpu.make_async_copy(ck_hbm.at[pg, :, kv, :], kbuf.at[slot, idx], sems.at[0, slot, idx]))
            copies.append(pltpu.make_async_copy(cv_hbm.at[pg, :, kv, :], vbuf.at[slot, idx], sems.at[1, slot, idx]))
        return copies

    n_slots = kbuf.shape[0]
    per_step = 4
    lookahead = n_slots - per_step
    assert n_items % per_step == 0 and lookahead % per_step == 0 and lookahead > 0
    for item in range(min(lookahead, n_items)):
        for c in page_copies(item, item % n_slots):
            c.start()

    def step(pair, carry):
        items = [pair * per_step + s for s in range(per_step)]
        for item in items:
            @pl.when(item + lookahead < n_items)
            def _(item=item):
                for c in page_copies(item + lookahead, (item + lookahead) % n_slots):
                    c.start()

        for item in items:
            for c in page_copies(item, item % n_slots):
                c.wait()
        for item in items:
            slot = item % n_slots
            b = item // n_heads
            h = item % n_heads
            n_keys = n_pages_sel * kbuf.shape[2]
            o_ref[b, h] = _sample_attend_one(q_ref[b, h], knew_ref[b, h // GROUP], vnew_ref[b, h // GROUP],
                                             kbuf[slot].reshape(n_keys, HEAD_DIM),
                                             vbuf[slot].reshape(n_keys, HEAD_DIM))
        return carry

    lax.fori_loop(0, n_items // per_step, step, 0)


def _sample_attend_one(q, k_new, v_new, k_sel, v_sel):
    q_scaled = q * (HEAD_DIM ** -0.5)
    qs = q_scaled.astype(BF16)
    qs8 = jnp.broadcast_to(q_scaled, (8, HEAD_DIM)).astype(BF16)
    s = lax.dot_general(qs8, k_sel.astype(BF16), (((1,), (1,)), ((), ())),
                        preferred_element_type=jnp.float32)
    knew = k_new.astype(BF16).astype(jnp.float32)
    s_own = jnp.sum(qs.astype(jnp.float32) * knew, axis=-1, keepdims=True)
    m = jnp.maximum(s_own, jnp.max(s[0:1, :], axis=-1, keepdims=True))
    p_own = jnp.exp(s_own - m)
    p = jnp.exp(s - m)
    inv = 1.0 / (p_own + jnp.sum(p[0:1, :], axis=-1, keepdims=True))
    vnew = v_new.astype(BF16).astype(jnp.float32)
    pv = jnp.dot((p * inv).astype(BF16), v_sel.astype(BF16), preferred_element_type=jnp.float32)
    return (p_own * inv).astype(BF16).astype(jnp.float32) * vnew + pv[0:1, :]


def sample_attend(q_s4, k_new4, v_new4, cache_k, cache_v, page_table, sel_flat):
    page = cache_k.shape[1]
    pages_per_block = MOBA_BLOCK // page
    n_pages_sel = MOBA_TOP_K * pages_per_block

    def whole(a):
        return pl.BlockSpec(a.shape, lambda i, pt, sel: (0,) * a.ndim)

    return pl.pallas_call(
        functools.partial(_sample_attend_kernel, pages_per_block=pages_per_block),
        grid_spec=pltpu.PrefetchScalarGridSpec(
            num_scalar_prefetch=2,
            grid=(1,),
            in_specs=[whole(q_s4), whole(k_new4), whole(v_new4),
                      pl.BlockSpec(memory_space=pl.ANY), pl.BlockSpec(memory_space=pl.ANY)],
            out_specs=whole(q_s4),
            scratch_shapes=[pltpu.VMEM((GATHER_SLOTS, n_pages_sel, page, HEAD_DIM), jnp.float32),
                            pltpu.VMEM((GATHER_SLOTS, n_pages_sel, page, HEAD_DIM), jnp.float32),
                            pltpu.SemaphoreType.DMA((2, GATHER_SLOTS, n_pages_sel))]),
        out_shape=jax.ShapeDtypeStruct(q_s4.shape, jnp.float32),
        compiler_params=_cparams(("arbitrary",)),
        name="sample_attend",
    )(page_table, sel_flat, q_s4, k_new4, v_new4, cache_k, cache_v)


def _rope_tables(positions):
    inv_freq = ROPE_THETA ** (-jnp.arange(ROT_HALF, dtype=jnp.float32) * 2.0 / ROT_DIM)
    ang = positions.astype(jnp.float32)[:, None] * inv_freq[None, :]
    cos, sin = jnp.cos(ang), jnp.sin(ang)
    rest = HEAD_DIM - ROT_DIM
    ones = jnp.ones((positions.shape[0], rest), jnp.float32)
    zeros = jnp.zeros((positions.shape[0], rest), jnp.float32)
    return (jnp.concatenate([cos, cos, ones], axis=1), jnp.concatenate([-sin, sin, zeros], axis=1))


def kernel(x_prompt, x_sample, state_conv, cache_k, cache_v, page_table, a_norm, a_w_in, a_conv_w, a_w_out,
           kv_norm, w_k, w_v, b_norm, w_q, w_o, ffn_norm, w_gate, w_up, w_down, final_norm):
    n_batch, seq, d_model = x_prompt.shape
    n_sample = x_sample.shape[0]
    assert x_sample.shape[1] == 1 and n_sample <= ROW_PAD
    assert a_w_in.shape[0] == 1 and w_q.shape[0] == 1, "one short-conv layer then one MoBA layer"
    _, page, n_kv, _ = cache_k.shape
    n_heads = d_model // HEAD_DIM
    past_len = page_table.shape[1] * page
    assert past_len % MOBA_BLOCK == 0 and past_len // MOBA_BLOCK >= MOBA_TOP_K
    rows_p = n_batch * seq
    rows = rows_p + ROW_PAD
    sample_blk = rows_p // ROW_PAD

    tm_down, tm_big = (_row_tile(rows, t) for t in (832, 1664))
    tr_p, tm_p = _row_tile(rows_p, 256), _row_tile(rows_p, 1024)

    x_sample_pad = jnp.concatenate([x_sample.reshape(n_sample, d_model),
                                    jnp.zeros((ROW_PAD - n_sample, d_model), x_sample.dtype)], axis=0)
    pos_p = jnp.tile(jnp.arange(seq, dtype=jnp.int32), n_batch)
    pos_s = jnp.concatenate([jnp.full((n_sample,), past_len, jnp.int32),
                             jnp.zeros((ROW_PAD - n_sample,), jnp.int32)])
    rope_p, rope_s = _rope_tables(pos_p), _rope_tables(pos_s)
    rope_all = tuple(jnp.concatenate([p, s], axis=0) for p, s in zip(rope_p, rope_s))

    def ffn(h, h_scaled, h_ssq, layer, next_gains):
        act, w_down_bf16 = matmul(h_scaled, [w_gate, w_up], layer=layer, tm=tm_big, tn=256, x_buffers=1,
                                  mode="swiglu", out_dtype=BF16, x_ssq=h_ssq, side_cast=(w_down, layer),
                                  name="ffn_gate_up")
        return matmul(act, [w_down_bf16], tm=tm_down, tn=256, mode="residual", extra=(h,),
                      next_gains=next_gains, split_rows=False, name="ffn_down")

    xn, x = rmsnorm_in(x_prompt.reshape(rows_p, d_model), x_sample_pad, a_norm, rows_per_step=tr_p)
    g, conv_prompt, conv_sample = conv_in(xn, a_w_in, a_conv_w[0], state_conv[0], n_batch=n_batch, seq=seq,
                                          n_sample=n_sample, tm=_row_tile(rows, 1040), tc=256)
    h, h_scaled, h_ssq = matmul(g, [a_w_out], layer=0, tm=tm_big, tn=512, x_buffers=1, mode="residual",
                                extra=(x,), next_gains=ffn_norm[0:1], name="conv_out")
    h, kv_in, q_in, h_ssq = ffn(h, h_scaled, h_ssq, 0, jnp.stack([kv_norm, b_norm[0]]))

    k_p, v_p = matmul(kv_in, [w_k, w_v], tm=tm_p, tn=256, mode="rotary_plain", extra=rope_p, x_ssq=h_ssq,
                      name="kv_proj", n_row_blocks=rows_p // tm_p)
    k_s, v_s = matmul(kv_in, [w_k, w_v], tm=ROW_PAD, tn=256, mode="rotary_plain", extra=rope_s, x_ssq=h_ssq,
                      name="kv_proj_sample", row_block0=sample_blk, n_row_blocks=1)
    q = matmul(q_in, [w_q], layer=0, tm=tm_big, tn=512, x_buffers=1, mode="rotary", extra=rope_all,
               x_ssq=h_ssq, name="q_proj")

    kmean = paged_block_means(cache_k, page_table)
    q_s = q[rows_p:rows_p + n_sample].reshape(n_sample, n_heads, HEAD_DIM)
    sel = sample_topk(q_s, kmean.transpose(0, 2, 1, 3))[:, :, :MOBA_TOP_K].reshape(-1)
    k_s = k_s[:n_sample].reshape(n_sample, n_kv, 1, HEAD_DIM)
    v_s = v_s[:n_sample].reshape(n_sample, n_kv, 1, HEAD_DIM)
    attn_s = sample_attend(q_s.reshape(n_sample, n_heads, 1, HEAD_DIM), k_s, v_s, cache_k, cache_v,
                           page_table, sel)
    attn_s = jnp.concatenate([attn_s.reshape(n_sample, d_model),
                              jnp.zeros((ROW_PAD - n_sample, d_model), jnp.float32)], axis=0)
    attn = moba_attention(q, k_p, v_p, attn_s, n_batch=n_batch, seq=seq)

    h, h_scaled, h_ssq = matmul(attn, [w_o], layer=0, tm=tm_big, tn=512, x_buffers=1, mode="residual",
                                extra=(h,), next_gains=ffn_norm[1:2], name="attn_out")
    h = ffn(h, h_scaled, h_ssq, 1, None)
    y_p, = rmsnorm(h, final_norm[None, :], [jnp.float32], rows_per_step=tr_p, n_row_blocks=rows_p // tr_p)
    y_s, = rmsnorm(h, final_norm[None, :], [jnp.float32], rows_per_step=ROW_PAD, row_block0=sample_blk,
                   n_row_blocks=1)

    return (y_p.reshape(n_batch, seq, d_model), y_s[:n_sample].reshape(n_sample, 1, d_model),
            conv_prompt[None], conv_sample[None],
            k_p.reshape(n_batch, seq, n_kv, HEAD_DIM), v_p.reshape(n_batch, seq, n_kv, HEAD_DIM),
            k_s.reshape(n_sample, 1, n_kv, HEAD_DIM), v_s.reshape(n_sample, 1, n_kv, HEAD_DIM))
```

```python
import functools

import jax
import jax.numpy as jnp
from jax import lax
from jax.experimental import pallas as pl
from jax.experimental.pallas import tpu as pltpu

HEAD_DIM = 128
GROUP = 4
ROT_DIM = HEAD_DIM // 4
ROT_HALF = ROT_DIM // 2
ROPE_THETA = 500000.0
MOBA_BLOCK = 256
MOBA_TOP_K = 3
CONV_WIDTH = 3
RMS_EPS = 1e-6
NEG_INF = -1e30

LANES = 128
BF16_SUBLANES = 16
VMEM_LIMIT_BYTES = 58 * 1024 * 1024

ROW_PAD = 128
BF16 = jnp.bfloat16


def _row_tile(rows, target):
    return max(t for t in range(BF16_SUBLANES, target + 1, BF16_SUBLANES) if rows % t == 0)


def _cparams(semantics):
    return pltpu.CompilerParams(dimension_semantics=semantics, vmem_limit_bytes=VMEM_LIMIT_BYTES)


def _rmsnorm_kernel(x_ref, g_ref, *o_refs):
    x = x_ref[...]
    xn = x * lax.rsqrt(jnp.mean(x * x, axis=-1, keepdims=True) + RMS_EPS)
    for i, o_ref in enumerate(o_refs):
        o_ref[...] = (xn * g_ref[i:i + 1, :]).astype(o_ref.dtype)


def rmsnorm(x, gains, out_dtypes, *, rows_per_step, row_block0=0, n_row_blocks=None):
    rows, d = x.shape
    n_out = gains.shape[0]
    if n_row_blocks is None:
        n_row_blocks = rows // rows_per_step
    out_spec = pl.BlockSpec((rows_per_step, d), lambda r: (r, 0))
    return pl.pallas_call(
        _rmsnorm_kernel,
        grid=(n_row_blocks,),
        in_specs=[pl.BlockSpec((rows_per_step, d), lambda r: (r + row_block0, 0)),
                  pl.BlockSpec((n_out, d), lambda r: (0, 0))],
        out_specs=[out_spec] * n_out,
        out_shape=[jax.ShapeDtypeStruct((n_row_blocks * rows_per_step, d), dt) for dt in out_dtypes],
        compiler_params=_cparams(("parallel",)),
        name="rmsnorm",
    )(x, gains)


def _rmsnorm_in_kernel(xp_ref, xs_ref, g_ref, xn_ref, xc_ref, *, n_prompt_steps):
    step = pl.program_id(0)

    def emit(x, rows):
        xn = x * lax.rsqrt(jnp.mean(x * x, axis=-1, keepdims=True) + RMS_EPS)
        xn_ref[rows, :] = (xn * g_ref[...]).astype(xn_ref.dtype)
        xc_ref[rows, :] = x

    @pl.when(step < n_prompt_steps)
    def _():
        emit(xp_ref[...], slice(None))

    @pl.when(step == n_prompt_steps)
    def _():
        emit(xs_ref[...], slice(0, ROW_PAD))


def rmsnorm_in(x_prompt2d, x_sample_pad, gain, *, rows_per_step):
    rows_p, d = x_prompt2d.shape
    n_prompt_steps = rows_p // rows_per_step
    assert rows_p % rows_per_step == 0 and ROW_PAD <= rows_per_step
    rows = rows_p + ROW_PAD
    out_spec = pl.BlockSpec((rows_per_step, d), lambda r: (r, 0))
    return pl.pallas_call(
        functools.partial(_rmsnorm_in_kernel, n_prompt_steps=n_prompt_steps),
        grid=(n_prompt_steps + 1,),
        in_specs=[pl.BlockSpec((rows_per_step, d), lambda r: (jnp.minimum(r, n_prompt_steps - 1), 0)),
                  pl.BlockSpec((ROW_PAD, d), lambda r: (0, 0)),
                  pl.BlockSpec((1, d), lambda r: (0, 0))],
        out_specs=[out_spec, out_spec],
        out_shape=[jax.ShapeDtypeStruct((rows, d), BF16), jax.ShapeDtypeStruct((rows, d), jnp.float32)],
        compiler_params=_cparams(("arbitrary",)),
        name="rmsnorm_in",
    )(x_prompt2d, x_sample_pad, gain)


def _rotary_epilogue(y, cos, sin_signed):
    lane = lax.broadcasted_iota(jnp.int32, (y.shape[0], HEAD_DIM), 1)
    first_half = lane < ROT_HALF
    outs = []
    for h in range(y.shape[1] // HEAD_DIM):
        yh = y[:, h * HEAD_DIM:(h + 1) * HEAD_DIM]
        from_above = pltpu.roll(yh, HEAD_DIM - ROT_HALF, axis=1)
        from_below = pltpu.roll(yh, ROT_HALF, axis=1)
        partner = jnp.where(first_half, from_above, from_below)
        outs.append(yh * cos + partner * sin_signed)
    return outs[0] if len(outs) == 1 else jnp.concatenate(outs, axis=1)


def _row_pieces(tm, n_pieces):
    step = max(tm // n_pieces // BF16_SUBLANES * BF16_SUBLANES, BF16_SUBLANES)
    bounds = [min(p * step, tm) for p in range(n_pieces)] + [tm]
    return [slice(lo, hi) for lo, hi in zip(bounds[:-1], bounds[1:]) if hi > lo]


def _matmul_kernel(*refs, n_w, mode, normed_x, n_gains, side_cast, row_pieces):
    refs = list(refs)
    x_ref = refs.pop(0)
    w_refs = [refs.pop(0) for _ in range(n_w)]
    ssq_in_ref = refs.pop(0) if normed_x else None
    res_ref = refs.pop(0) if mode == "residual" else None
    gains_ref = refs.pop(0) if n_gains else None
    rope_refs = [refs.pop(0) for _ in range(2)] if mode in ("rotary", "rotary_plain") else None
    side_in_ref = refs.pop(0) if side_cast else None
    o_refs = [refs.pop(0) for _ in range(2 if mode == "rotary_plain" else 1)]
    scaled_refs = [refs.pop(0) for _ in range(n_gains)]
    ssq_out_ref = refs.pop(0) if n_gains else None
    side_out_ref = refs.pop(0) if side_cast else None
    assert not refs
    i, j = pl.program_id(0), pl.program_id(1)

    tm, k_dim = x_ref.shape
    weights = [w_ref[...].astype(x_ref.dtype) for w_ref in w_refs]
    ssq_parts = []
    for rows in _row_pieces(tm, row_pieces):
        x = x_ref[rows, :]
        vals = [jnp.dot(x, w, preferred_element_type=jnp.float32) for w in weights]
        if normed_x:
            inv_rms = lax.rsqrt(ssq_in_ref[rows, :] * (1.0 / k_dim) + RMS_EPS)
            inv_rms = jnp.concatenate([inv_rms] * (vals[0].shape[1] // LANES), axis=1)
            vals = [v * inv_rms for v in vals]

        if mode == "swiglu":
            g, u = vals
            ys = [(g * jax.nn.sigmoid(g)) * u]
        elif mode == "residual":
            ys = [res_ref[rows, :] + vals[0]]
        elif mode == "rotary":
            ys = [_rotary_epilogue(vals[0], rope_refs[0][rows, :], rope_refs[1][rows, :])]
        elif mode == "rotary_plain":
            ys = [_rotary_epilogue(vals[0], rope_refs[0][rows, :], rope_refs[1][rows, :]), vals[1]]
        else:
            ys = vals
        for o_ref, y in zip(o_refs, ys):
            o_ref[rows, :] = y.astype(o_ref.dtype)

        if n_gains:
            h = ys[0]
            for n, scaled_ref in enumerate(scaled_refs):
                scaled_ref[rows, :] = (h * gains_ref[n:n + 1, :]).astype(scaled_ref.dtype)
            ssq_parts.append((rows, jnp.broadcast_to(jnp.sum(h * h, axis=1, keepdims=True),
                                                     (h.shape[0], LANES))))

    if n_gains:
        @pl.when(j == 0)
        def _():
            for rows, part in ssq_parts:
                ssq_out_ref[rows, :] = part

        @pl.when(j > 0)
        def _():
            for rows, part in ssq_parts:
                ssq_out_ref[rows, :] += part

    if side_cast:
        @pl.when(i == 0)
        def _():
            side_out_ref[...] = side_in_ref[...].astype(side_out_ref.dtype)


def matmul(x, ws, *, tm, tn, layer=None, mode="plain", extra=(), out_dtype=jnp.float32, name="matmul",
           row_block0=0, n_row_blocks=None, x_buffers=2, x_ssq=None, next_gains=None, side_cast=None,
           row_pieces=2):
    m, k_dim = x.shape
    n = ws[0].shape[-1]
    tn = min(tn, n)
    n_j = n // tn
    if n_row_blocks is None:
        n_row_blocks = m // tm
    rows_out = n_row_blocks * tm
    assert n % tn == 0 and (row_block0 + n_row_blocks) * tm <= m
    assert next_gains is None or mode == "residual"
    n_gains = 0 if next_gains is None else next_gains.shape[0]

    out_tile = pl.BlockSpec((tm, tn), lambda i, j: (i, j))

    def per_row(width):
        return pl.BlockSpec((tm, width), lambda i, j: (i, 0))

    x_mode = {} if x_buffers == 2 else {"pipeline_mode": pl.Buffered(x_buffers)}
    operands = [x]
    in_specs = [pl.BlockSpec((tm, k_dim), lambda i, j: (i + row_block0, 0), **x_mode)]
    operands += ws
    if layer is None:
        in_specs += [pl.BlockSpec((k_dim, tn), lambda i, j: (0, j)) for _ in ws]
    else:
        in_specs += [pl.BlockSpec((None, k_dim, tn), lambda i, j: (layer, 0, j)) for _ in ws]
    if x_ssq is not None:
        operands.append(x_ssq)
        in_specs.append(pl.BlockSpec((tm, LANES), lambda i, j: (i + row_block0, 0)))
    if mode == "residual":
        operands.append(extra[0])
        in_specs.append(out_tile)
    if n_gains:
        operands.append(next_gains)
        in_specs.append(pl.BlockSpec((n_gains, tn), lambda i, j: (0, j)))
    if mode in ("rotary", "rotary_plain"):
        operands += extra
        in_specs += [per_row(HEAD_DIM)] * 2

    n_main = 2 if mode == "rotary_plain" else 1
    out_specs = [out_tile] * n_main
    out_shapes = [jax.ShapeDtypeStruct((rows_out, n), out_dtype)] * n_main
    if n_gains:
        out_specs += [out_tile] * n_gains + [per_row(LANES)]
        out_shapes += [jax.ShapeDtypeStruct((rows_out, n), BF16)] * n_gains
        out_shapes.append(jax.ShapeDtypeStruct((rows_out, LANES), jnp.float32))
    if side_cast is not None:
        side_w, side_layer = side_cast
        _, side_rows, side_cols = side_w.shape
        slab = side_rows // n_j
        assert side_rows % n_j == 0 and slab % BF16_SUBLANES == 0

        def slab_index(i, j):
            return jnp.where(i == 0, j, n_j - 1)

        operands.append(side_w)
        in_specs.append(pl.BlockSpec((None, slab, side_cols), lambda i, j: (side_layer, slab_index(i, j), 0)))
        out_specs.append(pl.BlockSpec((slab, side_cols), lambda i, j: (slab_index(i, j), 0)))
        out_shapes.append(jax.ShapeDtypeStruct((side_rows, side_cols), BF16))

    outs = pl.pallas_call(
        functools.partial(_matmul_kernel, n_w=len(ws), mode=mode, normed_x=x_ssq is not None,
                          n_gains=n_gains, side_cast=side_cast is not None, row_pieces=row_pieces),
        grid=(n_row_blocks, n_j),
        in_specs=in_specs,
        out_specs=out_specs,
        out_shape=out_shapes,
        compiler_params=_cparams(("arbitrary" if side_cast is not None else "parallel",
                                  "arbitrary" if (n_gains or side_cast is not None) else "parallel")),
        name=name,
    )(*operands)
    return outs if len(outs) > 1 else outs[0]


CONV_ROW_PIECES = 2


def _conv_in_kernel(x_ref, wb_ref, wc_ref, wx_ref, st0_ref, st1_ref, cw_ref, g_ref, state_ref, us_ref,
                    carry_ref, *, seq, n_batch, row_pieces):
    i = pl.program_id(1)
    tm = x_ref.shape[0]
    halo = CONV_WIDTH - 1

    @pl.when(i == 0)
    def _():
        carry_ref[...] = jnp.zeros(carry_ref.shape, carry_ref.dtype)

    start = (i * tm + seq - 1) // seq * seq - i * tm
    start = jnp.where(start + i * tm < n_batch * seq, start, tm)
    weights = [w_ref[...].astype(x_ref.dtype) for w_ref in (wb_ref, wc_ref, wx_ref)]
    before2, before1 = carry_ref[0:1, :], carry_ref[1:2, :]
    pieces = []
    for lo, hi in ((r.start, r.stop) for r in row_pieces):
        x = x_ref[lo:hi, :]
        b, c, x_in = [jnp.dot(x, w, preferred_element_type=jnp.float32) for w in weights]
        u = c * x_in
        t = lax.broadcasted_iota(jnp.int32, u.shape, 0) + lo
        u_prev1 = jnp.where(t == lo, before1, pltpu.roll(u, 1, axis=0))
        u_prev2 = jnp.where(t == lo, before2, jnp.where(t == lo + 1, before1, pltpu.roll(u, 2, axis=0)))
        u_prev1 = jnp.where(t == start, 0.0, u_prev1)
        u_prev2 = jnp.where(t == start, 0.0, jnp.where(t == start + 1, 0.0, u_prev2))
        conv = cw_ref[0:1, :] * u_prev2 + cw_ref[1:2, :] * u_prev1 + cw_ref[2:3, :] * u
        g_ref[lo:hi, :] = (b * conv).astype(g_ref.dtype)
        before2, before1 = u[hi - lo - 2:hi - lo - 1, :], u[hi - lo - 1:, :]
        pieces.append((lo, hi, b, u))
    carry_ref[0:1, :] = before2
    carry_ref[1:2, :] = before1

    def rows_of(values, first, count):
        lo, _, b, u = next(p for p in pieces if p[0] <= first and first + count <= p[1])
        return {"b": b, "u": u}[values][first - lo:first - lo + count, :]

    for n in range(n_batch):
        tile, row = divmod((n + 1) * seq - halo, tm)

        @pl.when(i == tile)
        def _(n=n, row=row):
            state_ref[n] = rows_of("u", row, halo)

    @pl.when(i == pl.num_programs(1) - 1)
    def _():
        us = rows_of("u", tm - ROW_PAD, ROW_PAD)
        conv_s = cw_ref[0:1, :] * st0_ref[...] + cw_ref[1:2, :] * st1_ref[...] + cw_ref[2:3, :] * us
        g_ref[tm - ROW_PAD:, :] = (rows_of("b", tm - ROW_PAD, ROW_PAD) * conv_s).astype(g_ref.dtype)
        us_ref[...] = us


def conv_in(xn, w_in, conv_w, state_sample, *, n_batch, seq, n_sample, tm, tc):
    rows, d = xn.shape
    n_c = d // tc
    halo = CONV_WIDTH - 1
    row_pieces = _row_pieces(tm, CONV_ROW_PIECES)

    def in_one_piece(first, count):
        return any(r.start <= first and first + count <= r.stop for r in row_pieces)

    assert rows % tm == 0 and tm <= seq and rows == n_batch * seq + ROW_PAD
    assert in_one_piece(tm - ROW_PAD, ROW_PAD)
    for n in range(1, n_batch + 1):
        assert (n * seq) % tm != tm - 1 and in_one_piece((n * seq - halo) % tm, halo)
    pad = jnp.zeros((ROW_PAD - n_sample, d), jnp.float32)
    st0 = jnp.concatenate([state_sample[:, 0, :], pad], axis=0)
    st1 = jnp.concatenate([state_sample[:, 1, :], pad], axis=0)

    def w_spec(gate):
        return pl.BlockSpec((None, d, tc), lambda j, i: (0, 0, gate * n_c + j))

    sample_spec = pl.BlockSpec((ROW_PAD, tc), lambda j, i: (0, j))
    g, st_prompt, u_sample = pl.pallas_call(
        functools.partial(_conv_in_kernel, seq=seq, n_batch=n_batch, row_pieces=row_pieces),
        grid=(n_c, rows // tm),
        in_specs=[pl.BlockSpec((tm, d), lambda j, i: (i, 0)), w_spec(0), w_spec(1), w_spec(2),
                  sample_spec, sample_spec, pl.BlockSpec((CONV_WIDTH, tc), lambda j, i: (0, j))],
        out_specs=[pl.BlockSpec((tm, tc), lambda j, i: (i, j)),
                   pl.BlockSpec((n_batch, halo, tc), lambda j, i: (0, 0, j)),
                   sample_spec],
        out_shape=[jax.ShapeDtypeStruct((rows, d), BF16),
                   jax.ShapeDtypeStruct((n_batch, halo, d), jnp.float32),
                   jax.ShapeDtypeStruct((ROW_PAD, d), jnp.float32)],
        scratch_shapes=[pltpu.VMEM((8, tc), jnp.float32)],
        compiler_params=_cparams(("parallel", "arbitrary")),
        name="conv_in",
    )(xn, w_in, w_in, w_in, st0, st1, conv_w)
    st_sample = jnp.stack([state_sample[:, 1, :], u_sample[:n_sample]], axis=1)
    return g, st_prompt, st_sample


MOBA_KV_PER_STEP = 2


def _moba_kernel(q_ref, k_ref, v_ref, sample_ref, o_ref, *scratch, n_blocks, n_prompt_tiles):
    tile = pl.program_id(1)

    @pl.when(tile < n_prompt_tiles)
    def _():
        _moba_prompt_tile(q_ref, k_ref, v_ref, o_ref, *scratch, own=tile % n_blocks, n_blocks=n_blocks)

    @pl.when(tile == n_prompt_tiles)
    def _():
        o_ref[0:ROW_PAD, :] = sample_ref[...].astype(o_ref.dtype)


def _moba_prompt_tile(q_ref, k_ref, v_ref, o_ref, kbf_ref, vt_ref, kmean_ref, qs_ref, sel_ref,
                      s_ref, m_ref, l_ref, acc_ref, *, own, n_blocks):
    blk = MOBA_BLOCK
    n_kv_step = kbf_ref.shape[0]
    n_heads_step = n_kv_step * GROUP
    nt_dims = (((1,), (1,)), ((), ()))

    @pl.when(own == 0)
    def _():
        for c in range(n_kv_step):
            cols = slice(c * HEAD_DIM, (c + 1) * HEAD_DIM)
            kbf_ref[c] = k_ref[:, cols].astype(BF16)
            for n in range(n_blocks):
                rows = slice(n * blk, (n + 1) * blk)
                kmean_ref[c, n:n + 1, :] = jnp.mean(k_ref[rows, cols], axis=0, keepdims=True)
                vt_ref[c, n] = jnp.transpose(v_ref[rows, cols]).astype(BF16)

    for g in range(n_heads_step):
        q = q_ref[:, g * HEAD_DIM:(g + 1) * HEAD_DIM]
        gate_t = lax.dot_general(kmean_ref[g // GROUP], q, nt_dims, precision=lax.Precision.HIGHEST,
                                 preferred_element_type=jnp.float32)
        n_iota = lax.broadcasted_iota(jnp.int32, gate_t.shape, 0)
        beaten_by = jnp.zeros(gate_t.shape, jnp.int32)
        for n in range(n_blocks):
            row = gate_t[n:n + 1, :]
            beats = jnp.logical_or(row > gate_t, jnp.logical_and(row == gate_t, n < n_iota))
            beaten_by = beaten_by + jnp.where(beats, 1, 0) * (n < own).astype(jnp.int32)
        sel_t = jnp.logical_and(beaten_by < MOBA_TOP_K, n_iota < own)
        sel_ref[g] = jnp.where(sel_t, 1.0, 0.0)
        qs_ref[g] = (q * (HEAD_DIM ** -0.5)).astype(BF16)
    m_ref[...] = jnp.full(m_ref.shape, -jnp.inf, jnp.float32)
    l_ref[...] = jnp.zeros(l_ref.shape, jnp.float32)
    acc_ref[...] = jnp.zeros(acc_ref.shape, jnp.float32)

    def in_pairs(n, do_blocks):
        def pair(p, carry):
            do_blocks([2 * p, 2 * p + 1])
            return carry

        lax.fori_loop(0, n // 2, pair, 0)

        @pl.when(n % 2 == 1)
        def _():
            do_blocks([n - 1])

    def score_blocks(js, keep_of_head):
        for g in range(n_heads_step):
            m = m_ref[g:g + 1, :]
            for j in js:
                key_rows = pl.ds(pl.multiple_of(j * blk, blk), blk)
                s_t = lax.dot_general(kbf_ref[g // GROUP, key_rows, :], qs_ref[g], nt_dims,
                                      preferred_element_type=jnp.float32)
                s_t = jnp.where(keep_of_head(g, j), s_t, NEG_INF)
                s_ref[g, j] = s_t
                m = jnp.maximum(m, jnp.max(s_t, axis=0, keepdims=True))
            m_ref[g:g + 1, :] = m

    in_pairs(own, lambda js: score_blocks(js, lambda g, j: sel_ref[g, pl.ds(j, 1), :] > 0.5))
    k_pos = lax.broadcasted_iota(jnp.int32, (blk, blk), 0)
    q_pos = lax.broadcasted_iota(jnp.int32, (blk, blk), 1)
    causal = k_pos <= q_pos
    score_blocks([own], lambda g, j: causal)

    def weigh_blocks(js):
        for g in range(n_heads_step):
            l_new, acc_new = l_ref[g:g + 1, :], acc_ref[g]
            for j in js:
                p_t = jnp.exp(s_ref[g, j] - m_ref[g:g + 1, :])
                l_new = l_new + jnp.sum(p_t, axis=0, keepdims=True)
                acc_new = acc_new + jnp.dot(vt_ref[g // GROUP, j], p_t.astype(BF16),
                                            preferred_element_type=jnp.float32)
            l_ref[g:g + 1, :] = l_new
            acc_ref[g] = acc_new

    in_pairs(own + 1, weigh_blocks)

    for g in range(n_heads_step):
        out_t = acc_ref[g] / l_ref[g:g + 1, :]
        o_ref[:, g * HEAD_DIM:(g + 1) * HEAD_DIM] = jnp.transpose(out_t).astype(o_ref.dtype)


def moba_attention(q, k, v, attn_sample, *, n_batch, seq):
    rows, d = q.shape
    n_kv = k.shape[1] // HEAD_DIM
    n_blocks = seq // MOBA_BLOCK
    n_prompt_tiles = n_batch * n_blocks
    kvs = MOBA_KV_PER_STEP if n_kv % MOBA_KV_PER_STEP == 0 else 1
    hs = kvs * GROUP
    q_spec = pl.BlockSpec((MOBA_BLOCK, hs * HEAD_DIM), lambda h, t: (jnp.minimum(t, n_prompt_tiles - 1), h))
    kv_spec = pl.BlockSpec((seq, kvs * HEAD_DIM), lambda h, t: (jnp.minimum(t // n_blocks, n_batch - 1), h))
    return pl.pallas_call(
        functools.partial(_moba_kernel, n_blocks=n_blocks, n_prompt_tiles=n_prompt_tiles),
        grid=(n_kv // kvs, n_prompt_tiles + 1),
        in_specs=[q_spec, kv_spec, kv_spec, pl.BlockSpec((ROW_PAD, hs * HEAD_DIM), lambda h, t: (0, h))],
        out_specs=pl.BlockSpec((MOBA_BLOCK, hs * HEAD_DIM), lambda h, t: (t, h)),
        out_shape=jax.ShapeDtypeStruct((rows, d), BF16),
        scratch_shapes=[pltpu.VMEM((kvs, seq, HEAD_DIM), BF16),
                        pltpu.VMEM((kvs, n_blocks, HEAD_DIM, MOBA_BLOCK), BF16),
                        pltpu.VMEM((kvs, n_blocks, HEAD_DIM), jnp.float32),
                        pltpu.VMEM((hs, MOBA_BLOCK, HEAD_DIM), BF16),
                        pltpu.VMEM((hs, n_blocks, MOBA_BLOCK), jnp.float32),
                        pltpu.VMEM((hs, n_blocks, MOBA_BLOCK, MOBA_BLOCK), jnp.float32),
                        pltpu.VMEM((hs, MOBA_BLOCK), jnp.float32),
                        pltpu.VMEM((hs, MOBA_BLOCK), jnp.float32),
                        pltpu.VMEM((hs, HEAD_DIM, MOBA_BLOCK), jnp.float32)],
        compiler_params=_cparams(("parallel", "arbitrary")),
        name="moba_attention",
    )(q, k, v, attn_sample)


PAGES_PER_STEP = 16
GATHER_SLOTS = 8


def _page_mean_kernel(pt_ref, *refs, pages_per_block):
    del pt_ref
    page_refs, o_ref = refs[:-1], refs[-1]
    n_out = len(page_refs) // pages_per_block
    page_rows = page_refs[0].shape[0]
    for n in range(n_out):
        total = jnp.sum(page_refs[n * pages_per_block][...], axis=0)
        for r in range(1, pages_per_block):
            total = total + jnp.sum(page_refs[n * pages_per_block + r][...], axis=0)
        o_ref[n] = total / (pages_per_block * page_rows)


def paged_block_means(cache_k, page_table):
    _, page, n_kv, _ = cache_k.shape
    n_seq, n_pages = page_table.shape
    pages_per_block = MOBA_BLOCK // page
    blocks_per_step = PAGES_PER_STEP // pages_per_block
    n_steps = n_pages // PAGES_PER_STEP
    in_specs = [pl.BlockSpec((None, page, n_kv, HEAD_DIM),
                             lambda b, s, pt, r=r: (pt[b, s * PAGES_PER_STEP + r], 0, 0, 0))
                for r in range(PAGES_PER_STEP)]
    return pl.pallas_call(
        functools.partial(_page_mean_kernel, pages_per_block=pages_per_block),
        grid_spec=pltpu.PrefetchScalarGridSpec(
            num_scalar_prefetch=1,
            grid=(n_seq, n_steps),
            in_specs=in_specs,
            out_specs=pl.BlockSpec((None, blocks_per_step, n_kv, HEAD_DIM), lambda b, s, pt: (b, s, 0, 0))),
        out_shape=jax.ShapeDtypeStruct((n_seq, n_pages // pages_per_block, n_kv, HEAD_DIM), jnp.float32),
        compiler_params=_cparams(("parallel", "arbitrary")),
        name="page_means",
    )(page_table, *([cache_k] * PAGES_PER_STEP))


def _sample_topk_kernel(q_ref, kmean_ref, sel_ref):
    q = q_ref[...]
    n_heads = q.shape[0]
    n_kv, n_blocks, _ = kmean_ref.shape
    head_kv = lax.broadcasted_iota(jnp.int32, (n_heads, n_blocks), 0) // GROUP
    gate = jnp.zeros((n_heads, n_blocks), jnp.float32)
    for kv in range(n_kv):
        g = lax.dot_general(q, kmean_ref[kv], (((1,), (1,)), ((), ())),
                            precision=lax.Precision.HIGHEST, preferred_element_type=jnp.float32)
        gate = jnp.where(head_kv == kv, g, gate)
    blk = lax.broadcasted_iota(jnp.int32, gate.shape, 1).astype(jnp.float32)
    lane = lax.broadcasted_iota(jnp.int32, sel_ref.shape, 1)
    sel = jnp.zeros(sel_ref.shape, jnp.float32)
    for s in range(MOBA_TOP_K):
        best = jnp.max(gate, axis=-1, keepdims=True)
        idx = jnp.min(jnp.where(gate == best, blk, float(n_blocks)), axis=-1, keepdims=True)
        sel = jnp.where(lane == s, idx, sel)
        gate = jnp.where(blk == idx, -jnp.inf, gate)
    sel_ref[...] = sel.astype(jnp.int32)


def sample_topk(q_s, kmean_t):
    n_seq, n_heads, _ = q_s.shape
    _, n_kv, n_blocks, _ = kmean_t.shape
    return pl.pallas_call(
        _sample_topk_kernel,
        grid=(n_seq,),
        in_specs=[pl.BlockSpec((None, n_heads, HEAD_DIM), lambda b: (b, 0, 0)),
                  pl.BlockSpec((None, n_kv, n_blocks, HEAD_DIM), lambda b: (b, 0, 0, 0))],
        out_specs=pl.BlockSpec((None, n_heads, LANES), lambda b: (b, 0, 0)),
        out_shape=jax.ShapeDtypeStruct((n_seq, n_heads, LANES), jnp.int32),
        compiler_params=_cparams(("parallel",)),
        name="sample_topk",
    )(q_s, kmean_t)


def _sample_attend_kernel(pt_ref, sel_ref, q_ref, knew_ref, vnew_ref, ck_hbm, cv_hbm, o_ref,
                          kbuf, vbuf, sems, *, pages_per_block):
    n_seq, n_heads, _, _ = q_ref.shape
    n_pages_sel = kbuf.shape[1]
    n_items = n_seq * n_heads

    def page_copies(item, slot):
        b = item // n_heads
        kv = (item % n_heads) // GROUP
        copies = []
        for idx in range(n_pages_sel):
            blk = sel_ref[item * MOBA_TOP_K + idx // pages_per_block]
            pg = pt_ref[b, blk * pages_per_block + idx % pages_per_block]
            copies.append(pltpu.make_async_copy(ck_hbm.at[pg, :, kv, :], kbuf.at[slot, idx], sems.at[0, slot, idx]))
            copies.append(pltpu.make_async_copy(cv_hbm.at[pg, :, kv, :], vbuf.at[slot, idx], sems.at[1, slot, idx]))
        return copies

    n_slots = kbuf.shape[0]
    per_step = 4
    lookahead = n_slots - per_step
    assert n_items % per_step == 0 and lookahead % per_step == 0 and lookahead > 0
    for item in range(min(lookahead, n_items)):
        for c in page_copies(item, item % n_slots):
            c.start()

    def step(pair, carry):
        items = [pair * per_step + s for s in range(per_step)]
        for item in items:
            @pl.when(item + lookahead < n_items)
            def _(item=item):
                for c in page_copies(item + lookahead, (item + lookahead) % n_slots):
                    c.start()

        for item in items:
            for c in page_copies(item, item % n_slots):
                c.wait()
        for item in items:
            slot = item % n_slots
            b = item // n_heads
            h = item % n_heads
            n_keys = n_pages_sel * kbuf.shape[2]
            o_ref[b, h] = _sample_attend_one(q_ref[b, h], knew_ref[b, h // GROUP], vnew_ref[b, h // GROUP],
                                             kbuf[slot].reshape(n_keys, HEAD_DIM),
                                             vbuf[slot].reshape(n_keys, HEAD_DIM))
        return carry

    lax.fori_loop(0, n_items // per_step, step, 0)


def _sample_attend_one(q, k_new, v_new, k_sel, v_sel):
    q_scaled = q * (HEAD_DIM ** -0.5)
    qs = q_scaled.astype(BF16)
    qs8 = jnp.broadcast_to(q_scaled, (8, HEAD_DIM)).astype(BF16)
    s = lax.dot_general(qs8, k_sel.astype(BF16), (((1,), (1,)), ((), ())),
                        preferred_element_type=jnp.float32)
    knew = k_new.astype(BF16).astype(jnp.float32)
    s_own = jnp.sum(qs.astype(jnp.float32) * knew, axis=-1, keepdims=True)
    m = jnp.maximum(s_own, jnp.max(s[0:1, :], axis=-1, keepdims=True))
    p_own = jnp.exp(s_own - m)
    p = jnp.exp(s - m)
    inv = 1.0 / (p_own + jnp.sum(p[0:1, :], axis=-1, keepdims=True))
    vnew = v_new.astype(BF16).astype(jnp.float32)
    pv = jnp.dot((p * inv).astype(BF16), v_sel.astype(BF16), preferred_element_type=jnp.float32)
    return (p_own * inv).astype(BF16).astype(jnp.float32) * vnew + pv[0:1, :]


def sample_attend(q_s4, k_new4, v_new4, cache_k, cache_v, page_table, sel_flat):
    page = cache_k.shape[1]
    pages_per_block = MOBA_BLOCK // page
    n_pages_sel = MOBA_TOP_K * pages_per_block

    def whole(a):
        return pl.BlockSpec(a.shape, lambda i, pt, sel: (0,) * a.ndim)

    return pl.pallas_call(
        functools.partial(_sample_attend_kernel, pages_per_block=pages_per_block),
        grid_spec=pltpu.PrefetchScalarGridSpec(
            num_scalar_prefetch=2,
            grid=(1,),
            in_specs=[whole(q_s4), whole(k_new4), whole(v_new4),
                      pl.BlockSpec(memory_space=pl.ANY), pl.BlockSpec(memory_space=pl.ANY)],
            out_specs=whole(q_s4),
            scratch_shapes=[pltpu.VMEM((GATHER_SLOTS, n_pages_sel, page, HEAD_DIM), jnp.float32),
                            pltpu.VMEM((GATHER_SLOTS, n_pages_sel, page, HEAD_DIM), jnp.float32),
                            pltpu.SemaphoreType.DMA((2, GATHER_SLOTS, n_pages_sel))]),
        out_shape=jax.ShapeDtypeStruct(q_s4.shape, jnp.float32),
        compiler_params=_cparams(("arbitrary",)),
        name="sample_attend",
    )(page_table, sel_flat, q_s4, k_new4, v_new4, cache_k, cache_v)


def _rope_tables(positions):
    inv_freq = ROPE_THETA ** (-jnp.arange(ROT_HALF, dtype=jnp.float32) * 2.0 / ROT_DIM)
    ang = positions.astype(jnp.float32)[:, None] * inv_freq[None, :]
    cos, sin = jnp.cos(ang), jnp.sin(ang)
    rest = HEAD_DIM - ROT_DIM
    ones = jnp.ones((positions.shape[0], rest), jnp.float32)
    zeros = jnp.zeros((positions.shape[0], rest), jnp.float32)
    return (jnp.concatenate([cos, cos, ones], axis=1), jnp.concatenate([-sin, sin, zeros], axis=1))


def kernel(x_prompt, x_sample, state_conv, cache_k, cache_v, page_table, a_norm, a_w_in, a_conv_w, a_w_out,
           kv_norm, w_k, w_v, b_norm, w_q, w_o, ffn_norm, w_gate, w_up, w_down, final_norm):
    n_batch, seq, d_model = x_prompt.shape
    n_sample = x_sample.shape[0]
    assert x_sample.shape[1] == 1 and n_sample <= ROW_PAD
    assert a_w_in.shape[0] == 1 and w_q.shape[0] == 1, "one short-conv layer then one MoBA layer"
    _, page, n_kv, _ = cache_k.shape
    n_heads = d_model // HEAD_DIM
    past_len = page_table.shape[1] * page
    assert past_len % MOBA_BLOCK == 0 and past_len // MOBA_BLOCK >= MOBA_TOP_K
    rows_p = n_batch * seq
    rows = rows_p + ROW_PAD
    sample_blk = rows_p // ROW_PAD

    tm_down, tm_big = (_row_tile(rows, t) for t in (832, 1664))
    tr_p, tm_p = _row_tile(rows_p, 256), _row_tile(rows_p, 1024)

    x_sample_pad = jnp.concatenate([x_sample.reshape(n_sample, d_model),
                                    jnp.zeros((ROW_PAD - n_sample, d_model), x_sample.dtype)], axis=0)
    pos_p = jnp.tile(jnp.arange(seq, dtype=jnp.int32), n_batch)
    pos_s = jnp.concatenate([jnp.full((n_sample,), past_len, jnp.int32),
                             jnp.zeros((ROW_PAD - n_sample,), jnp.int32)])
    rope_p, rope_s = _rope_tables(pos_p), _rope_tables(pos_s)
    rope_all = tuple(jnp.concatenate([p, s], axis=0) for p, s in zip(rope_p, rope_s))

    def ffn(h, h_scaled, h_ssq, layer, next_gains):
        act, w_down_bf16 = matmul(h_scaled, [w_gate, w_up], layer=layer, tm=_row_tile(rows, 2080), tn=256, x_buffers=1,
                                  mode="swiglu", out_dtype=BF16, x_ssq=h_ssq, side_cast=(w_down, layer),
                                  name="ffn_gate_up")
        return matmul(act, [w_down_bf16], tm=tm_down, tn=256, mode="residual", extra=(h,),
                      next_gains=next_gains, row_pieces=1, name="ffn_down")

    xn, x = rmsnorm_in(x_prompt.reshape(rows_p, d_model), x_sample_pad, a_norm, rows_per_step=tr_p)
    g, conv_prompt, conv_sample = conv_in(xn, a_w_in, a_conv_w[0], state_conv[0], n_batch=n_batch, seq=seq,
                                          n_sample=n_sample, tm=_row_tile(rows, 1040), tc=256)
    h, h_scaled, h_ssq = matmul(g, [a_w_out], layer=0, tm=tm_big, tn=512, x_buffers=1, mode="residual",
                                extra=(x,), next_gains=ffn_norm[0:1], name="conv_out")
    h, kv_in, q_in, h_ssq = ffn(h, h_scaled, h_ssq, 0, jnp.stack([kv_norm, b_norm[0]]))

    k_p, v_p = matmul(kv_in, [w_k, w_v], tm=tm_p, tn=256, mode="rotary_plain", extra=rope_p, x_ssq=h_ssq,
                      row_pieces=4, name="kv_proj", n_row_blocks=rows_p // tm_p)
    k_s, v_s = matmul(kv_in, [w_k, w_v], tm=ROW_PAD, tn=256, mode="rotary_plain", extra=rope_s, x_ssq=h_ssq,
                      name="kv_proj_sample", row_block0=sample_blk, n_row_blocks=1)
    q = matmul(q_in, [w_q], layer=0, tm=tm_big, tn=512, x_buffers=1, mode="rotary", extra=rope_all,
               x_ssq=h_ssq, row_pieces=4, name="q_proj")

    kmean = paged_block_means(cache_k, page_table)
    q_s = q[rows_p:rows_p + n_sample].reshape(n_sample, n_heads, HEAD_DIM)
    sel = sample_topk(q_s, kmean.transpose(0, 2, 1, 3))[:, :, :MOBA_TOP_K].reshape(-1)
    k_s = k_s[:n_sample].reshape(n_sample, n_kv, 1, HEAD_DIM)
    v_s = v_s[:n_sample].reshape(n_sample, n_kv, 1, HEAD_DIM)
    attn_s = sample_attend(q_s.reshape(n_sample, n_heads, 1, HEAD_DIM), k_s, v_s, cache_k, cache_v,
                           page_table, sel)
    attn_s = jnp.concatenate([attn_s.reshape(n_sample, d_model),
                              jnp.zeros((ROW_PAD - n_sample, d_model), jnp.float32)], axis=0)
    attn = moba_attention(q, k_p, v_p, attn_s, n_batch=n_batch, seq=seq)

    h, h_scaled, h_ssq = matmul(attn, [w_o], layer=0, tm=tm_big, tn=512, x_buffers=1, mode="residual",
                                extra=(h,), next_gains=ffn_norm[1:2], name="attn_out")
    h = ffn(h, h_scaled, h_ssq, 1, None)
    y_p, = rmsnorm(h, final_norm[None, :], [jnp.float32], rows_per_step=tr_p, n_row_blocks=rows_p // tr_p)
    y_s, = rmsnorm(h, final_norm[None, :], [jnp.float32], rows_per_step=ROW_PAD, row_block0=sample_blk,
                   n_row_blocks=1)

    return (y_p.reshape(n_batch, seq, d_model), y_s[:n_sample].reshape(n_sample, 1, d_model),
            conv_prompt[None], conv_sample[None],
            k_p.reshape(n_batch, seq, n_kv, HEAD_DIM), v_p.reshape(n_batch, seq, n_kv, HEAD_DIM),
            k_s.reshape(n_sample, 1, n_kv, HEAD_DIM), v_s.reshape(n_sample, 1, n_kv, HEAD_DIM))
```

```python
import functools

import jax
import jax.numpy as jnp
from jax import lax
from jax.experimental import pallas as pl
from jax.experimental.pallas import tpu as pltpu

HEAD_DIM = 128
GROUP = 4
ROT_DIM = HEAD_DIM // 4
ROT_HALF = ROT_DIM // 2
ROPE_THETA = 500000.0
MOBA_BLOCK = 256
MOBA_TOP_K = 3
CONV_WIDTH = 3
RMS_EPS = 1e-6
NEG_INF = -1e30

LANES = 128
BF16_SUBLANES = 16
VMEM_LIMIT_BYTES = 58 * 1024 * 1024

ROW_PAD = 128
BF16 = jnp.bfloat16


def _row_tile(rows, target):
    return max(t for t in range(BF16_SUBLANES, target + 1, BF16_SUBLANES) if rows % t == 0)


def _cparams(semantics):
    return pltpu.CompilerParams(dimension_semantics=semantics, vmem_limit_bytes=VMEM_LIMIT_BYTES)


def _rmsnorm_kernel(x_ref, g_ref, *o_refs):
    x = x_ref[...]
    xn = x * lax.rsqrt(jnp.mean(x * x, axis=-1, keepdims=True) + RMS_EPS)
    for i, o_ref in enumerate(o_refs):
        o_ref[...] = (xn * g_ref[i:i + 1, :]).astype(o_ref.dtype)


def rmsnorm(x, gains, out_dtypes, *, rows_per_step, row_block0=0, n_row_blocks=None):
    rows, d = x.shape
    n_out = gains.shape[0]
    if n_row_blocks is None:
        n_row_blocks = rows // rows_per_step
    out_spec = pl.BlockSpec((rows_per_step, d), lambda r: (r, 0))
    return pl.pallas_call(
        _rmsnorm_kernel,
        grid=(n_row_blocks,),
        in_specs=[pl.BlockSpec((rows_per_step, d), lambda r: (r + row_block0, 0)),
                  pl.BlockSpec((n_out, d), lambda r: (0, 0))],
        out_specs=[out_spec] * n_out,
        out_shape=[jax.ShapeDtypeStruct((n_row_blocks * rows_per_step, d), dt) for dt in out_dtypes],
        compiler_params=_cparams(("parallel",)),
        name="rmsnorm",
    )(x, gains)


def _rmsnorm_in_kernel(xp_ref, xs_ref, g_ref, xn_ref, xc_ref, *, n_prompt_steps):
    step = pl.program_id(0)

    def emit(x, rows):
        xn = x * lax.rsqrt(jnp.mean(x * x, axis=-1, keepdims=True) + RMS_EPS)
        xn_ref[rows, :] = (xn * g_ref[...]).astype(xn_ref.dtype)
        xc_ref[rows, :] = x

    @pl.when(step < n_prompt_steps)
    def _():
        emit(xp_ref[...], slice(None))

    @pl.when(step == n_prompt_steps)
    def _():
        emit(xs_ref[...], slice(0, ROW_PAD))


def rmsnorm_in(x_prompt2d, x_sample_pad, gain, *, rows_per_step):
    rows_p, d = x_prompt2d.shape
    n_prompt_steps = rows_p // rows_per_step
    assert rows_p % rows_per_step == 0 and ROW_PAD <= rows_per_step
    rows = rows_p + ROW_PAD
    out_spec = pl.BlockSpec((rows_per_step, d), lambda r: (r, 0))
    return pl.pallas_call(
        functools.partial(_rmsnorm_in_kernel, n_prompt_steps=n_prompt_steps),
        grid=(n_prompt_steps + 1,),
        in_specs=[pl.BlockSpec((rows_per_step, d), lambda r: (jnp.minimum(r, n_prompt_steps - 1), 0)),
                  pl.BlockSpec((ROW_PAD, d), lambda r: (0, 0)),
                  pl.BlockSpec((1, d), lambda r: (0, 0))],
        out_specs=[out_spec, out_spec],
        out_shape=[jax.ShapeDtypeStruct((rows, d), BF16), jax.ShapeDtypeStruct((rows, d), jnp.float32)],
        compiler_params=_cparams(("arbitrary",)),
        name="rmsnorm_in",
    )(x_prompt2d, x_sample_pad, gain)


def _rotary_epilogue(y, cos, sin_signed):
    lane = lax.broadcasted_iota(jnp.int32, (y.shape[0], HEAD_DIM), 1)
    first_half = lane < ROT_HALF
    outs = []
    for h in range(y.shape[1] // HEAD_DIM):
        yh = y[:, h * HEAD_DIM:(h + 1) * HEAD_DIM]
        from_above = pltpu.roll(yh, HEAD_DIM - ROT_HALF, axis=1)
        from_below = pltpu.roll(yh, ROT_HALF, axis=1)
        partner = jnp.where(first_half, from_above, from_below)
        outs.append(yh * cos + partner * sin_signed)
    return outs[0] if len(outs) == 1 else jnp.concatenate(outs, axis=1)


def _row_pieces(tm, n_pieces):
    step = max(tm // n_pieces // BF16_SUBLANES * BF16_SUBLANES, BF16_SUBLANES)
    bounds = [min(p * step, tm) for p in range(n_pieces)] + [tm]
    return [slice(lo, hi) for lo, hi in zip(bounds[:-1], bounds[1:]) if hi > lo]


def _matmul_kernel(*refs, n_w, mode, normed_x, n_gains, side_cast, row_pieces):
    refs = list(refs)
    x_ref = refs.pop(0)
    w_refs = [refs.pop(0) for _ in range(n_w)]
    ssq_in_ref = refs.pop(0) if normed_x else None
    res_ref = refs.pop(0) if mode == "residual" else None
    gains_ref = refs.pop(0) if n_gains else None
    rope_refs = [refs.pop(0) for _ in range(2)] if mode in ("rotary", "rotary_plain") else None
    side_in_ref = refs.pop(0) if side_cast else None
    o_refs = [refs.pop(0) for _ in range(2 if mode == "rotary_plain" else 1)]
    scaled_refs = [refs.pop(0) for _ in range(n_gains)]
    ssq_out_ref = refs.pop(0) if n_gains else None
    side_out_ref = refs.pop(0) if side_cast else None
    assert not refs
    i, j = pl.program_id(0), pl.program_id(1)

    tm, k_dim = x_ref.shape
    weights = [w_ref[...].astype(x_ref.dtype) for w_ref in w_refs]
    ssq_parts = []
    for rows in _row_pieces(tm, row_pieces):
        x = x_ref[rows, :]
        vals = [jnp.dot(x, w, preferred_element_type=jnp.float32) for w in weights]
        if normed_x:
            inv_rms = lax.rsqrt(ssq_in_ref[rows, :] * (1.0 / k_dim) + RMS_EPS)
            inv_rms = jnp.concatenate([inv_rms] * (vals[0].shape[1] // LANES), axis=1)
            vals = [v * inv_rms for v in vals]

        if mode == "swiglu":
            g, u = vals
            ys = [(g * jax.nn.sigmoid(g)) * u]
        elif mode == "residual":
            ys = [res_ref[rows, :] + vals[0]]
        elif mode == "rotary":
            ys = [_rotary_epilogue(vals[0], rope_refs[0][rows, :], rope_refs[1][rows, :])]
        elif mode == "rotary_plain":
            ys = [_rotary_epilogue(vals[0], rope_refs[0][rows, :], rope_refs[1][rows, :]), vals[1]]
        else:
            ys = vals
        for o_ref, y in zip(o_refs, ys):
            o_ref[rows, :] = y.astype(o_ref.dtype)

        if n_gains:
            h = ys[0]
            for n, scaled_ref in enumerate(scaled_refs):
                scaled_ref[rows, :] = (h * gains_ref[n:n + 1, :]).astype(scaled_ref.dtype)
            ssq_parts.append((rows, jnp.broadcast_to(jnp.sum(h * h, axis=1, keepdims=True),
                                                     (h.shape[0], LANES))))

    if n_gains:
        @pl.when(j == 0)
        def _():
            for rows, part in ssq_parts:
                ssq_out_ref[rows, :] = part

        @pl.when(j > 0)
        def _():
            for rows, part in ssq_parts:
                ssq_out_ref[rows, :] += part

    if side_cast:
        @pl.when(i == 0)
        def _():
            side_out_ref[...] = side_in_ref[...].astype(side_out_ref.dtype)


def matmul(x, ws, *, tm, tn, layer=None, mode="plain", extra=(), out_dtype=jnp.float32, name="matmul",
           row_block0=0, n_row_blocks=None, x_buffers=2, x_ssq=None, next_gains=None, side_cast=None,
           row_pieces=2):
    m, k_dim = x.shape
    n = ws[0].shape[-1]
    tn = min(tn, n)
    n_j = n // tn
    if n_row_blocks is None:
        n_row_blocks = m // tm
    rows_out = n_row_blocks * tm
    assert n % tn == 0 and (row_block0 + n_row_blocks) * tm <= m
    assert next_gains is None or mode == "residual"
    n_gains = 0 if next_gains is None else next_gains.shape[0]

    out_tile = pl.BlockSpec((tm, tn), lambda i, j: (i, j))

    def per_row(width):
        return pl.BlockSpec((tm, width), lambda i, j: (i, 0))

    x_mode = {} if x_buffers == 2 else {"pipeline_mode": pl.Buffered(x_buffers)}
    operands = [x]
    in_specs = [pl.BlockSpec((tm, k_dim), lambda i, j: (i + row_block0, 0), **x_mode)]
    operands += ws
    if layer is None:
        in_specs += [pl.BlockSpec((k_dim, tn), lambda i, j: (0, j)) for _ in ws]
    else:
        in_specs += [pl.BlockSpec((None, k_dim, tn), lambda i, j: (layer, 0, j)) for _ in ws]
    if x_ssq is not None:
        operands.append(x_ssq)
        in_specs.append(pl.BlockSpec((tm, LANES), lambda i, j: (i + row_block0, 0)))
    if mode == "residual":
        operands.append(extra[0])
        in_specs.append(out_tile)
    if n_gains:
        operands.append(next_gains)
        in_specs.append(pl.BlockSpec((n_gains, tn), lambda i, j: (0, j)))
    if mode in ("rotary", "rotary_plain"):
        operands += extra
        in_specs += [per_row(HEAD_DIM)] * 2

    n_main = 2 if mode == "rotary_plain" else 1
    out_specs = [out_tile] * n_main
    out_shapes = [jax.ShapeDtypeStruct((rows_out, n), out_dtype)] * n_main
    if n_gains:
        out_specs += [out_tile] * n_gains + [per_row(LANES)]
        out_shapes += [jax.ShapeDtypeStruct((rows_out, n), BF16)] * n_gains
        out_shapes.append(jax.ShapeDtypeStruct((rows_out, LANES), jnp.float32))
    if side_cast is not None:
        side_w, side_layer = side_cast
        _, side_rows, side_cols = side_w.shape
        slab = side_rows // n_j
        assert side_rows % n_j == 0 and slab % BF16_SUBLANES == 0

        def slab_index(i, j):
            return jnp.where(i == 0, j, n_j - 1)

        operands.append(side_w)
        in_specs.append(pl.BlockSpec((None, slab, side_cols), lambda i, j: (side_layer, slab_index(i, j), 0)))
        out_specs.append(pl.BlockSpec((slab, side_cols), lambda i, j: (slab_index(i, j), 0)))
        out_shapes.append(jax.ShapeDtypeStruct((side_rows, side_cols), BF16))

    outs = pl.pallas_call(
        functools.partial(_matmul_kernel, n_w=len(ws), mode=mode, normed_x=x_ssq is not None,
                          n_gains=n_gains, side_cast=side_cast is not None, row_pieces=row_pieces),
        grid=(n_row_blocks, n_j),
        in_specs=in_specs,
        out_specs=out_specs,
        out_shape=out_shapes,
        compiler_params=_cparams(("arbitrary" if side_cast is not None else "parallel",
                                  "arbitrary" if (n_gains or side_cast is not None) else "parallel")),
        name=name,
    )(*operands)
    return outs if len(outs) > 1 else outs[0]


CONV_ROW_PIECES = 2


def _conv_in_kernel(x_ref, wb_ref, wc_ref, wx_ref, st0_ref, st1_ref, cw_ref, g_ref, state_ref, us_ref,
                    carry_ref, *, seq, n_batch, row_pieces):
    i = pl.program_id(1)
    tm = x_ref.shape[0]
    halo = CONV_WIDTH - 1

    @pl.when(i == 0)
    def _():
        carry_ref[...] = jnp.zeros(carry_ref.shape, carry_ref.dtype)

    start = (i * tm + seq - 1) // seq * seq - i * tm
    start = jnp.where(start + i * tm < n_batch * seq, start, tm)
    weights = [w_ref[...].astype(x_ref.dtype) for w_ref in (wb_ref, wc_ref, wx_ref)]
    before2, before1 = carry_ref[0:1, :], carry_ref[1:2, :]
    pieces = []
    for lo, hi in ((r.start, r.stop) for r in row_pieces):
        x = x_ref[lo:hi, :]
        b, c, x_in = [jnp.dot(x, w, preferred_element_type=jnp.float32) for w in weights]
        u = c * x_in
        t = lax.broadcasted_iota(jnp.int32, u.shape, 0) + lo
        u_prev1 = jnp.where(t == lo, before1, pltpu.roll(u, 1, axis=0))
        u_prev2 = jnp.where(t == lo, before2, jnp.where(t == lo + 1, before1, pltpu.roll(u, 2, axis=0)))
        u_prev1 = jnp.where(t == start, 0.0, u_prev1)
        u_prev2 = jnp.where(t == start, 0.0, jnp.where(t == start + 1, 0.0, u_prev2))
        conv = cw_ref[0:1, :] * u_prev2 + cw_ref[1:2, :] * u_prev1 + cw_ref[2:3, :] * u
        g_ref[lo:hi, :] = (b * conv).astype(g_ref.dtype)
        before2, before1 = u[hi - lo - 2:hi - lo - 1, :], u[hi - lo - 1:, :]
        pieces.append((lo, hi, b, u))
    carry_ref[0:1, :] = before2
    carry_ref[1:2, :] = before1

    def rows_of(values, first, count):
        lo, _, b, u = next(p for p in pieces if p[0] <= first and first + count <= p[1])
        return {"b": b, "u": u}[values][first - lo:first - lo + count, :]

    for n in range(n_batch):
        tile, row = divmod((n + 1) * seq - halo, tm)

        @pl.when(i == tile)
        def _(n=n, row=row):
            state_ref[n] = rows_of("u", row, halo)

    @pl.when(i == pl.num_programs(1) - 1)
    def _():
        us = rows_of("u", tm - ROW_PAD, ROW_PAD)
        conv_s = cw_ref[0:1, :] * st0_ref[...] + cw_ref[1:2, :] * st1_ref[...] + cw_ref[2:3, :] * us
        g_ref[tm - ROW_PAD:, :] = (rows_of("b", tm - ROW_PAD, ROW_PAD) * conv_s).astype(g_ref.dtype)
        us_ref[...] = us


def conv_in(xn, w_in, conv_w, state_sample, *, n_batch, seq, n_sample, tm, tc):
    rows, d = xn.shape
    n_c = d // tc
    halo = CONV_WIDTH - 1
    row_pieces = _row_pieces(tm, CONV_ROW_PIECES)

    def in_one_piece(first, count):
        return any(r.start <= first and first + count <= r.stop for r in row_pieces)

    assert rows % tm == 0 and tm <= seq and rows == n_batch * seq + ROW_PAD
    assert in_one_piece(tm - ROW_PAD, ROW_PAD)
    for n in range(1, n_batch + 1):
        assert (n * seq) % tm != tm - 1 and in_one_piece((n * seq - halo) % tm, halo)
    pad = jnp.zeros((ROW_PAD - n_sample, d), jnp.float32)
    st0 = jnp.concatenate([state_sample[:, 0, :], pad], axis=0)
    st1 = jnp.concatenate([state_sample[:, 1, :], pad], axis=0)

    def w_spec(gate):
        return pl.BlockSpec((None, d, tc), lambda j, i: (0, 0, gate * n_c + j))

    sample_spec = pl.BlockSpec((ROW_PAD, tc), lambda j, i: (0, j))
    g, st_prompt, u_sample = pl.pallas_call(
        functools.partial(_conv_in_kernel, seq=seq, n_batch=n_batch, row_pieces=row_pieces),
        grid=(n_c, rows // tm),
        in_specs=[pl.BlockSpec((tm, d), lambda j, i: (i, 0)), w_spec(0), w_spec(1), w_spec(2),
                  sample_spec, sample_spec, pl.BlockSpec((CONV_WIDTH, tc), lambda j, i: (0, j))],
        out_specs=[pl.BlockSpec((tm, tc), lambda j, i: (i, j)),
                   pl.BlockSpec((n_batch, halo, tc), lambda j, i: (0, 0, j)),
                   sample_spec],
        out_shape=[jax.ShapeDtypeStruct((rows, d), BF16),
                   jax.ShapeDtypeStruct((n_batch, halo, d), jnp.float32),
                   jax.ShapeDtypeStruct((ROW_PAD, d), jnp.float32)],
        scratch_shapes=[pltpu.VMEM((8, tc), jnp.float32)],
        compiler_params=_cparams(("parallel", "arbitrary")),
        name="conv_in",
    )(xn, w_in, w_in, w_in, st0, st1, conv_w)
    st_sample = jnp.stack([state_sample[:, 1, :], u_sample[:n_sample]], axis=1)
    return g, st_prompt, st_sample


MOBA_KV_PER_STEP = 2


def _moba_kernel(q_ref, k_ref, v_ref, sample_ref, o_ref, *scratch, n_blocks, n_prompt_tiles):
    tile = pl.program_id(1)

    @pl.when(tile < n_prompt_tiles)
    def _():
        _moba_prompt_tile(q_ref, k_ref, v_ref, o_ref, *scratch, own=tile % n_blocks, n_blocks=n_blocks)

    @pl.when(tile == n_prompt_tiles)
    def _():
        o_ref[0:ROW_PAD, :] = sample_ref[...].astype(o_ref.dtype)


def _moba_prompt_tile(q_ref, k_ref, v_ref, o_ref, kbf_ref, vt_ref, kmean_ref, qs_ref, sel_ref,
                      s_ref, m_ref, l_ref, acc_ref, *, own, n_blocks):
    blk = MOBA_BLOCK
    n_kv_step = kbf_ref.shape[0]
    n_heads_step = n_kv_step * GROUP
    nt_dims = (((1,), (1,)), ((), ()))

    @pl.when(own == 0)
    def _():
        for c in range(n_kv_step):
            cols = slice(c * HEAD_DIM, (c + 1) * HEAD_DIM)
            kbf_ref[c] = k_ref[:, cols].astype(BF16)
            for n in range(n_blocks):
                rows = slice(n * blk, (n + 1) * blk)
                kmean_ref[c, n:n + 1, :] = jnp.mean(k_ref[rows, cols], axis=0, keepdims=True)
                vt_ref[c, n] = jnp.transpose(v_ref[rows, cols]).astype(BF16)

    for g in range(n_heads_step):
        q = q_ref[:, g * HEAD_DIM:(g + 1) * HEAD_DIM]
        gate_t = lax.dot_general(kmean_ref[g // GROUP], q, nt_dims, precision=lax.Precision.HIGHEST,
                                 preferred_element_type=jnp.float32)
        n_iota = lax.broadcasted_iota(jnp.int32, gate_t.shape, 0)
        beaten_by = jnp.zeros(gate_t.shape, jnp.int32)
        for n in range(n_blocks):
            row = gate_t[n:n + 1, :]
            beats = jnp.logical_or(row > gate_t, jnp.logical_and(row == gate_t, n < n_iota))
            beaten_by = beaten_by + jnp.where(beats, 1, 0) * (n < own).astype(jnp.int32)
        sel_t = jnp.logical_and(beaten_by < MOBA_TOP_K, n_iota < own)
        sel_ref[g] = jnp.where(sel_t, 1.0, 0.0)
        qs_ref[g] = (q * (HEAD_DIM ** -0.5)).astype(BF16)
    m_ref[...] = jnp.full(m_ref.shape, -jnp.inf, jnp.float32)
    l_ref[...] = jnp.zeros(l_ref.shape, jnp.float32)
    acc_ref[...] = jnp.zeros(acc_ref.shape, jnp.float32)

    def in_pairs(n, do_blocks):
        def pair(p, carry):
            do_blocks([2 * p, 2 * p + 1])
            return carry

        lax.fori_loop(0, n // 2, pair, 0)

        @pl.when(n % 2 == 1)
        def _():
            do_blocks([n - 1])

    def score_blocks(js, keep_of_head):
        for g in range(n_heads_step):
            m = m_ref[g:g + 1, :]
            for j in js:
                key_rows = pl.ds(pl.multiple_of(j * blk, blk), blk)
                s_t = lax.dot_general(kbf_ref[g // GROUP, key_rows, :], qs_ref[g], nt_dims,
                                      preferred_element_type=jnp.float32)
                s_t = jnp.where(keep_of_head(g, j), s_t, NEG_INF)
                s_ref[g, j] = s_t
                m = jnp.maximum(m, jnp.max(s_t, axis=0, keepdims=True))
            m_ref[g:g + 1, :] = m

    in_pairs(own, lambda js: score_blocks(js, lambda g, j: sel_ref[g, pl.ds(j, 1), :] > 0.5))
    k_pos = lax.broadcasted_iota(jnp.int32, (blk, blk), 0)
    q_pos = lax.broadcasted_iota(jnp.int32, (blk, blk), 1)
    causal = k_pos <= q_pos
    score_blocks([own], lambda g, j: causal)

    def weigh_blocks(js):
        for g in range(n_heads_step):
            l_new, acc_new = l_ref[g:g + 1, :], acc_ref[g]
            for j in js:
                p_t = jnp.exp(s_ref[g, j] - m_ref[g:g + 1, :])
                l_new = l_new + jnp.sum(p_t, axis=0, keepdims=True)
                acc_new = acc_new + jnp.dot(vt_ref[g // GROUP, j], p_t.astype(BF16),
                                            preferred_element_type=jnp.float32)
            l_ref[g:g + 1, :] = l_new
            acc_ref[g] = acc_new

    in_pairs(own + 1, weigh_blocks)

    for g in range(n_heads_step):
        out_t = acc_ref[g] / l_ref[g:g + 1, :]
        o_ref[:, g * HEAD_DIM:(g + 1) * HEAD_DIM] = jnp.transpose(out_t).astype(o_ref.dtype)


def moba_attention(q, k, v, attn_sample, *, n_batch, seq):
    rows, d = q.shape
    n_kv = k.shape[1] // HEAD_DIM
    n_blocks = seq // MOBA_BLOCK
    n_prompt_tiles = n_batch * n_blocks
    kvs = MOBA_KV_PER_STEP if n_kv % MOBA_KV_PER_STEP == 0 else 1
    hs = kvs * GROUP
    q_spec = pl.BlockSpec((MOBA_BLOCK, hs * HEAD_DIM), lambda h, t: (jnp.minimum(t, n_prompt_tiles - 1), h))
    kv_spec = pl.BlockSpec((seq, kvs * HEAD_DIM), lambda h, t: (jnp.minimum(t // n_blocks, n_batch - 1), h))
    return pl.pallas_call(
        functools.partial(_moba_kernel, n_blocks=n_blocks, n_prompt_tiles=n_prompt_tiles),
        grid=(n_kv // kvs, n_prompt_tiles + 1),
        in_specs=[q_spec, kv_spec, kv_spec, pl.BlockSpec((ROW_PAD, hs * HEAD_DIM), lambda h, t: (0, h))],
        out_specs=pl.BlockSpec((MOBA_BLOCK, hs * HEAD_DIM), lambda h, t: (t, h)),
        out_shape=jax.ShapeDtypeStruct((rows, d), BF16),
        scratch_shapes=[pltpu.VMEM((kvs, seq, HEAD_DIM), BF16),
                        pltpu.VMEM((kvs, n_blocks, HEAD_DIM, MOBA_BLOCK), BF16),
                        pltpu.VMEM((kvs, n_blocks, HEAD_DIM), jnp.float32),
                        pltpu.VMEM((hs, MOBA_BLOCK, HEAD_DIM), BF16),
                        pltpu.VMEM((hs, n_blocks, MOBA_BLOCK), jnp.float32),
                        pltpu.VMEM((hs, n_blocks, MOBA_BLOCK, MOBA_BLOCK), jnp.float32),
                        pltpu.VMEM((hs, MOBA_BLOCK), jnp.float32),
                        pltpu.VMEM((hs, MOBA_BLOCK), jnp.float32),
                        pltpu.VMEM((hs, HEAD_DIM, MOBA_BLOCK), jnp.float32)],
        compiler_params=_cparams(("parallel", "arbitrary")),
        name="moba_attention",
    )(q, k, v, attn_sample)


PAGES_PER_STEP = 16
GATHER_SLOTS = 8


def _page_mean_kernel(pt_ref, *refs, pages_per_block):
    del pt_ref
    page_refs, o_ref = refs[:-1], refs[-1]
    n_out = len(page_refs) // pages_per_block
    page_rows = page_refs[0].shape[0]
    for n in range(n_out):
        total = jnp.sum(page_refs[n * pages_per_block][...], axis=0)
        for r in range(1, pages_per_block):
            total = total + jnp.sum(page_refs[n * pages_per_block + r][...], axis=0)
        o_ref[n] = total / (pages_per_block * page_rows)


def paged_block_means(cache_k, page_table):
    _, page, n_kv, _ = cache_k.shape
    n_seq, n_pages = page_table.shape
    pages_per_block = MOBA_BLOCK // page
    blocks_per_step = PAGES_PER_STEP // pages_per_block
    n_steps = n_pages // PAGES_PER_STEP
    in_specs = [pl.BlockSpec((None, page, n_kv, HEAD_DIM),
                             lambda b, s, pt, r=r: (pt[b, s * PAGES_PER_STEP + r], 0, 0, 0))
                for r in range(PAGES_PER_STEP)]
    return pl.pallas_call(
        functools.partial(_page_mean_kernel, pages_per_block=pages_per_block),
        grid_spec=pltpu.PrefetchScalarGridSpec(
            num_scalar_prefetch=1,
            grid=(n_seq, n_steps),
            in_specs=in_specs,
            out_specs=pl.BlockSpec((None, blocks_per_step, n_kv, HEAD_DIM), lambda b, s, pt: (b, s, 0, 0))),
        out_shape=jax.ShapeDtypeStruct((n_seq, n_pages // pages_per_block, n_kv, HEAD_DIM), jnp.float32),
        compiler_params=_cparams(("parallel", "arbitrary")),
        name="page_means",
    )(page_table, *([cache_k] * PAGES_PER_STEP))


def _sample_topk_kernel(q_ref, kmean_ref, sel_ref):
    q = q_ref[...]
    n_heads = q.shape[0]
    n_kv, n_blocks, _ = kmean_ref.shape
    head_kv = lax.broadcasted_iota(jnp.int32, (n_heads, n_blocks), 0) // GROUP
    gate = jnp.zeros((n_heads, n_blocks), jnp.float32)
    for kv in range(n_kv):
        g = lax.dot_general(q, kmean_ref[kv], (((1,), (1,)), ((), ())),
                            precision=lax.Precision.HIGHEST, preferred_element_type=jnp.float32)
        gate = jnp.where(head_kv == kv, g, gate)
    blk = lax.broadcasted_iota(jnp.int32, gate.shape, 1).astype(jnp.float32)
    lane = lax.broadcasted_iota(jnp.int32, sel_ref.shape, 1)
    sel = jnp.zeros(sel_ref.shape, jnp.float32)
    for s in range(MOBA_TOP_K):
        best = jnp.max(gate, axis=-1, keepdims=True)
        idx = jnp.min(jnp.where(gate == best, blk, float(n_blocks)), axis=-1, keepdims=True)
        sel = jnp.where(lane == s, idx, sel)
        gate = jnp.where(blk == idx, -jnp.inf, gate)
    sel_ref[...] = sel.astype(jnp.int32)


def sample_topk(q_s, kmean_t):
    n_seq, n_heads, _ = q_s.shape
    _, n_kv, n_blocks, _ = kmean_t.shape
    return pl.pallas_call(
        _sample_topk_kernel,
        grid=(n_seq,),
        in_specs=[pl.BlockSpec((None, n_heads, HEAD_DIM), lambda b: (b, 0, 0)),
                  pl.BlockSpec((None, n_kv, n_blocks, HEAD_DIM), lambda b: (b, 0, 0, 0))],
        out_specs=pl.BlockSpec((None, n_heads, LANES), lambda b: (b, 0, 0)),
        out_shape=jax.ShapeDtypeStruct((n_seq, n_heads, LANES), jnp.int32),
        compiler_params=_cparams(("parallel",)),
        name="sample_topk",
    )(q_s, kmean_t)


def _sample_attend_kernel(pt_ref, sel_ref, q_ref, knew_ref, vnew_ref, ck_hbm, cv_hbm, o_ref,
                          kbuf, vbuf, sems, *, pages_per_block):
    n_seq, n_heads, _, _ = q_ref.shape
    n_pages_sel = kbuf.shape[1]
    n_items = n_seq * n_heads

    def page_copies(item, slot):
        b = item // n_heads
        kv = (item % n_heads) // GROUP
        copies = []
        for idx in range(n_pages_sel):
            blk = sel_ref[item * MOBA_TOP_K + idx // pages_per_block]
            pg = pt_ref[b, blk * pages_per_block + idx % pages_per_block]
            copies.append(pltpu.make_async_copy(ck_hbm.at[pg, :, kv, :], kbuf.at[slot, idx], sems.at[0, slot, idx]))
            copies.append(pltpu.make_async_copy(cv_hbm.at[pg, :, kv, :], vbuf.at[slot, idx], sems.at[1, slot, idx]))
        return copies

    n_slots = kbuf.shape[0]
    per_step = 4
    lookahead = n_slots - per_step
    assert n_items % per_step == 0 and lookahead % per_step == 0 and lookahead > 0
    for item in range(min(lookahead, n_items)):
        for c in page_copies(item, item % n_slots):
            c.start()

    def step(pair, carry):
        items = [pair * per_step + s for s in range(per_step)]
        for item in items:
            @pl.when(item + lookahead < n_items)
            def _(item=item):
                for c in page_copies(item + lookahead, (item + lookahead) % n_slots):
                    c.start()

        for item in items:
            for c in page_copies(item, item % n_slots):
                c.wait()
        for item in items:
            slot = item % n_slots
            b = item // n_heads
            h = item % n_heads
            n_keys = n_pages_sel * kbuf.shape[2]
            o_ref[b, h] = _sample_attend_one(q_ref[b, h], knew_ref[b, h // GROUP], vnew_ref[b, h // GROUP],
                                             kbuf[slot].reshape(n_keys, HEAD_DIM),
                                             vbuf[slot].reshape(n_keys, HEAD_DIM))
        return carry

    lax.fori_loop(0, n_items // per_step, step, 0)


def _sample_attend_one(q, k_new, v_new, k_sel, v_sel):
    q_scaled = q * (HEAD_DIM ** -0.5)
    qs = q_scaled.astype(BF16)
    qs8 = jnp.broadcast_to(q_scaled, (8, HEAD_DIM)).astype(BF16)
    s = lax.dot_general(qs8, k_sel.astype(BF16), (((1,), (1,)), ((), ())),
                        preferred_element_type=jnp.float32)
    knew = k_new.astype(BF16).astype(jnp.float32)
    s_own = jnp.sum(qs.astype(jnp.float32) * knew, axis=-1, keepdims=True)
    m = jnp.maximum(s_own, jnp.max(s[0:1, :], axis=-1, keepdims=True))
    p_own = jnp.exp(s_own - m)
    p = jnp.exp(s - m)
    inv = 1.0 / (p_own + jnp.sum(p[0:1, :], axis=-1, keepdims=True))
    vnew = v_new.astype(BF16).astype(jnp.float32)
    pv = jnp.dot((p * inv).astype(BF16), v_sel.astype(BF16), preferred_element_type=jnp.float32)
    return (p_own * inv).astype(BF16).astype(jnp.float32) * vnew + pv[0:1, :]


def sample_attend(q_s4, k_new4, v_new4, cache_k, cache_v, page_table, sel_flat):
    page = cache_k.shape[1]
    pages_per_block = MOBA_BLOCK // page
    n_pages_sel = MOBA_TOP_K * pages_per_block

    def whole(a):
        return pl.BlockSpec(a.shape, lambda i, pt, sel: (0,) * a.ndim)

    return pl.pallas_call(
        functools.partial(_sample_attend_kernel, pages_per_block=pages_per_block),
        grid_spec=pltpu.PrefetchScalarGridSpec(
            num_scalar_prefetch=2,
            grid=(1,),
            in_specs=[whole(q_s4), whole(k_new4), whole(v_new4),
                      pl.BlockSpec(memory_space=pl.ANY), pl.BlockSpec(memory_space=pl.ANY)],
            out_specs=whole(q_s4),
            scratch_shapes=[pltpu.VMEM((GATHER_SLOTS, n_pages_sel, page, HEAD_DIM), jnp.float32),
                            pltpu.VMEM((GATHER_SLOTS, n_pages_sel, page, HEAD_DIM), jnp.float32),
                            pltpu.SemaphoreType.DMA((2, GATHER_SLOTS, n_pages_sel))]),
        out_shape=jax.ShapeDtypeStruct(q_s4.shape, jnp.float32),
        compiler_params=_cparams(("arbitrary",)),
        name="sample_attend",
    )(page_table, sel_flat, q_s4, k_new4, v_new4, cache_k, cache_v)


def _rope_tables(positions):
    inv_freq = ROPE_THETA ** (-jnp.arange(ROT_HALF, dtype=jnp.float32) * 2.0 / ROT_DIM)
    ang = positions.astype(jnp.float32)[:, None] * inv_freq[None, :]
    cos, sin = jnp.cos(ang), jnp.sin(ang)
    rest = HEAD_DIM - ROT_DIM
    ones = jnp.ones((positions.shape[0], rest), jnp.float32)
    zeros = jnp.zeros((positions.shape[0], rest), jnp.float32)
    return (jnp.concatenate([cos, cos, ones], axis=1), jnp.concatenate([-sin, sin, zeros], axis=1))


def kernel(x_prompt, x_sample, state_conv, cache_k, cache_v, page_table, a_norm, a_w_in, a_conv_w, a_w_out,
           kv_norm, w_k, w_v, b_norm, w_q, w_o, ffn_norm, w_gate, w_up, w_down, final_norm):
    n_batch, seq, d_model = x_prompt.shape
    n_sample = x_sample.shape[0]
    assert x_sample.shape[1] == 1 and n_sample <= ROW_PAD
    assert a_w_in.shape[0] == 1 and w_q.shape[0] == 1, "one short-conv layer then one MoBA layer"
    _, page, n_kv, _ = cache_k.shape
    n_heads = d_model // HEAD_DIM
    past_len = page_table.shape[1] * page
    assert past_len % MOBA_BLOCK == 0 and past_len // MOBA_BLOCK >= MOBA_TOP_K
    rows_p = n_batch * seq
    rows = rows_p + ROW_PAD
    sample_blk = rows_p // ROW_PAD

    tm_down, tm_big = (_row_tile(rows, t) for t in (832, 1664))
    tr_p, tm_p = _row_tile(rows_p, 256), _row_tile(rows_p, 1024)

    x_sample_pad = jnp.concatenate([x_sample.reshape(n_sample, d_model),
                                    jnp.zeros((ROW_PAD - n_sample, d_model), x_sample.dtype)], axis=0)
    pos_p = jnp.tile(jnp.arange(seq, dtype=jnp.int32), n_batch)
    pos_s = jnp.concatenate([jnp.full((n_sample,), past_len, jnp.int32),
                             jnp.zeros((ROW_PAD - n_sample,), jnp.int32)])
    rope_p, rope_s = _rope_tables(pos_p), _rope_tables(pos_s)
    rope_all = tuple(jnp.concatenate([p, s], axis=0) for p, s in zip(rope_p, rope_s))

    def ffn(h, h_scaled, h_ssq, layer, next_gains):
        act, w_down_bf16 = matmul(h_scaled, [w_gate, w_up], layer=layer, tm=tm_big, tn=256, x_buffers=1, row_pieces=4,
                                  mode="swiglu", out_dtype=BF16, x_ssq=h_ssq, side_cast=(w_down, layer),
                                  name="ffn_gate_up")
        return matmul(act, [w_down_bf16], tm=tm_down, tn=256, mode="residual", extra=(h,),
                      next_gains=next_gains, row_pieces=1, name="ffn_down")

    xn, x = rmsnorm_in(x_prompt.reshape(rows_p, d_model), x_sample_pad, a_norm, rows_per_step=tr_p)
    g, conv_prompt, conv_sample = conv_in(xn, a_w_in, a_conv_w[0], state_conv[0], n_batch=n_batch, seq=seq,
                                          n_sample=n_sample, tm=_row_tile(rows, 1040), tc=256)
    h, h_scaled, h_ssq = matmul(g, [a_w_out], layer=0, tm=tm_big, tn=512, x_buffers=1, mode="residual",
                                extra=(x,), next_gains=ffn_norm[0:1], name="conv_out")
    h, kv_in, q_in, h_ssq = ffn(h, h_scaled, h_ssq, 0, jnp.stack([kv_norm, b_norm[0]]))

    k_p, v_p = matmul(kv_in, [w_k, w_v], tm=tm_p, tn=256, mode="rotary_plain", extra=rope_p, x_ssq=h_ssq,
                      row_pieces=4, name="kv_proj", n_row_blocks=rows_p // tm_p)
    k_s, v_s = matmul(kv_in, [w_k, w_v], tm=ROW_PAD, tn=256, mode="rotary_plain", extra=rope_s, x_ssq=h_ssq,
                      name="kv_proj_sample", row_block0=sample_blk, n_row_blocks=1)
    q = matmul(q_in, [w_q], layer=0, tm=tm_big, tn=512, x_buffers=1, mode="rotary", extra=rope_all,
               x_ssq=h_ssq, row_pieces=4, name="q_proj")

    kmean = paged_block_means(cache_k, page_table)
    q_s = q[rows_p:rows_p + n_sample].reshape(n_sample, n_heads, HEAD_DIM)
    sel = sample_topk(q_s, kmean.transpose(0, 2, 1, 3))[:, :, :MOBA_TOP_K].reshape(-1)
    k_s = k_s[:n_sample].reshape(n_sample, n_kv, 1, HEAD_DIM)
    v_s = v_s[:n_sample].reshape(n_sample, n_kv, 1, HEAD_DIM)
    attn_s = sample_attend(q_s.reshape(n_sample, n_heads, 1, HEAD_DIM), k_s, v_s, cache_k, cache_v,
                           page_table, sel)
    attn_s = jnp.concatenate([attn_s.reshape(n_sample, d_model),
                              jnp.zeros((ROW_PAD - n_sample, d_model), jnp.float32)], axis=0)
    attn = moba_attention(q, k_p, v_p, attn_s, n_batch=n_batch, seq=seq)

    h, h_scaled, h_ssq = matmul(attn, [w_o], layer=0, tm=tm_big, tn=512, x_buffers=1, mode="residual",
                                extra=(h,), next_gains=ffn_norm[1:2], name="attn_out")
    h = ffn(h, h_scaled, h_ssq, 1, None)
    y_p, = rmsnorm(h, final_norm[None, :], [jnp.float32], rows_per_step=tr_p, n_row_blocks=rows_p // tr_p)
    y_s, = rmsnorm(h, final_norm[None, :], [jnp.float32], rows_per_step=ROW_PAD, row_block0=sample_blk,
                   n_row_blocks=1)

    return (y_p.reshape(n_batch, seq, d_model), y_s[:n_sample].reshape(n_sample, 1, d_model),
            conv_prompt[None], conv_sample[None],
            k_p.reshape(n_batch, seq, n_kv, HEAD_DIM), v_p.reshape(n_batch, seq, n_kv, HEAD_DIM),
            k_s.reshape(n_sample, 1, n_kv, HEAD_DIM), v_s.reshape(n_sample, 1, n_kv, HEAD_DIM))
```
